```python
import jax, jax.numpy as jnp
from jax import lax
import numpy as np

D_MODEL = 1024
BATCH = 4
SEQ = 8192
DEPTH = 4
DEC_BATCH = 16
DEC_SEQ = 64
PAST_LEN = 2048

CHUNK = 64
N_MIXERS = 2
N_POOL = (DEPTH + 1) // 2
N_HGRN = DEPTH // 2
POOL_WINDOWS = (2, 4, 8, 16)
POOL_GROUPS = len(POOL_WINDOWS)
POOL_GW = D_MODEL // POOL_GROUPS
POOL_CACHE = min(max(POOL_WINDOWS) - 1, PAST_LEN)
HG_HEAD_DIM = 128
HG_HEADS = D_MODEL // HG_HEAD_DIM
D_FF = 2816
CONV_W = 3
EPS = 1e-6

kernel_name = 'pool_hgrn2_convffn_stream_step'

F32 = jnp.float32


def rmsnorm(x, g):
    xf = x.astype(F32)
    y = xf * lax.rsqrt(jnp.mean(xf * xf, axis=-1, keepdims=True) + EPS)
    return (y * g.astype(F32)).astype(x.dtype)


def pool_mix(h, hist, pos0, w, scale):
    L = h.shape[1]
    P = hist.shape[1]
    ext = jnp.concatenate([hist.astype(h.dtype), h], axis=1)
    c = jnp.pad(jnp.cumsum(ext.astype(F32), axis=1), ((0, 0), (1, 0), (0, 0)))
    end = jnp.arange(L) + P + 1
    pos = jnp.arange(L) + pos0
    outs = []
    for g, win in enumerate(POOL_WINDOWS):
        sl = slice(g * POOL_GW, (g + 1) * POOL_GW)
        cg = c[..., sl]
        lo = jnp.maximum(end - win, 0)
        wsum = cg[:, P + 1:] - jnp.take(cg, lo, axis=1)
        div = jnp.minimum(win, pos + 1).astype(F32)
        d = wsum / div[None, :, None] - h[..., sl].astype(F32)
        outs.append(jnp.einsum('bld,de->ble', d.astype(h.dtype), w[g]))
    y = jnp.concatenate(outs, axis=-1) * scale
    return y, ext[:, -POOL_CACHE:]


def hgrn_chunk(S, inp):
    q, k, v, g = inp
    C = q.shape[2]
    b = jnp.cumsum(g, axis=2)
    causal = jnp.tril(jnp.ones((C, C), dtype=bool))
    diff = b[:, :, :, None, :] - b[:, :, None, :, :]
    decay = jnp.exp(jnp.where(causal[None, None, :, :, None], diff, -jnp.inf))
    A = jnp.einsum('bhtk,bhsk,bhtsk->bhts', q, k, decay)
    o = jnp.einsum('bhtk,bhkv->bhtv', q * jnp.exp(b), S) + jnp.einsum('bhts,bhsv->bhtv', A, v)
    bC = b[:, :, -1:, :]
    S_new = jnp.exp(bC[:, :, 0, :])[..., None] * S + jnp.einsum('bhsk,bhsv->bhkv', k * jnp.exp(bC - b), v)
    return S_new, o


def hgrn_mix(h, S0, w_in, lb, gn, w_out):
    B, L, _ = h.shape
    proj = jnp.einsum('bld,de->ble', h, w_in).astype(F32)
    zq, zf, zi, zg = jnp.split(proj, 4, axis=-1)
    lbf = lb.astype(F32)
    q = jax.nn.silu(zq)
    log_f = jnp.logaddexp(jnp.log(lbf), jnp.log1p(-lbf) + jax.nn.log_sigmoid(zf))
    k = (1.0 - lbf) * jax.nn.sigmoid(-zf)
    C = CHUNK if L % CHUNK == 0 else L
    nc = L // C

    def to_chunks(a):
        return a.reshape(B, nc, C, HG_HEADS, HG_HEAD_DIM).transpose(1, 0, 3, 2, 4)

    S_fin, o = lax.scan(hgrn_chunk, S0.astype(F32), (to_chunks(q), to_chunks(k), to_chunks(zi), to_chunks(log_f)))
    o = o.transpose(1, 0, 3, 2, 4).reshape(B, L, HG_HEADS, HG_HEAD_DIM)
    o = o * lax.rsqrt(jnp.mean(o * o, axis=-1, keepdims=True) + EPS) * gn.astype(F32).reshape(HG_HEADS, HG_HEAD_DIM)
    o = o.reshape(B, L, D_MODEL) * jax.nn.silu(zg)
    y = jnp.einsum('bld,de->ble', o.astype(h.dtype), w_out)
    return y, S_fin


def conv_ffn(h, hist, w_up, cw, cb, w_down):
    L = h.shape[1]
    up = jnp.einsum('bld,df->blf', h, w_up)
    gate_pre, val = jnp.split(up, 2, axis=-1)
    ext = jnp.concatenate([hist.astype(up.dtype), gate_pre], axis=1)
    conv = cb
    for j in range(CONV_W):
        conv = conv + cw[j] * ext[:, j:j + L]
    hid = jax.nn.silu(conv.astype(F32)) * val.astype(F32)
    y = jnp.einsum('blf,fd->bld', hid.astype(h.dtype), w_down)
    return y, ext[:, -(CONV_W - 1):]


def trunk(x, pool_hist, hgrn_S, conv_hist, pos0, lb_all, norm_mix_g, pool_w, pool_scale, hgrn_w_in,
          hgrn_norm_g, hgrn_w_out, norm_ffn_g, ffn_w_up, ffn_conv_w, ffn_conv_b, ffn_w_down, norm_out_g):
    new_pool, new_hgrn, new_conv = [], [], []
    for i in range(DEPTH):
        hn = rmsnorm(x, norm_mix_g[i])
        j = i // N_MIXERS
        if i % N_MIXERS == 0:
            y, st = pool_mix(hn, pool_hist[j], pos0, pool_w[j], pool_scale[j])
            new_pool.append(st)
        else:
            y, st = hgrn_mix(hn, hgrn_S[j], hgrn_w_in[j], lb_all[i], hgrn_norm_g[j], hgrn_w_out[j])
            new_hgrn.append(st)
        x = x + y.astype(x.dtype)
        hn = rmsnorm(x, norm_ffn_g[i])
        y, cs = conv_ffn(hn, conv_hist[i], ffn_w_up[i], ffn_conv_w[i], ffn_conv_b[i], ffn_w_down[i])
        new_conv.append(cs)
        x = x + y.astype(x.dtype)
    x = rmsnorm(x, norm_out_g)
    return x, jnp.stack(new_pool), jnp.stack(new_hgrn), jnp.stack(new_conv)


def setup_inputs(seed: int = 0) -> dict:
    key = jax.random.key(seed)
    ks = jax.random.split(key, 18)
    n = jax.random.normal
    D, F = D_MODEL, D_FF
    return {
        'x_prompt': n(ks[0], (BATCH, SEQ, D), F32),
        'x_sample': n(ks[1], (DEC_BATCH, DEC_SEQ, D), F32),
        'state_pool': n(ks[2], (N_POOL, DEC_BATCH, POOL_CACHE, D), F32),
        'state_hgrn': 0.5 * n(ks[3], (N_HGRN, DEC_BATCH, HG_HEADS, HG_HEAD_DIM, HG_HEAD_DIM), F32),
        'state_ffn_conv': n(ks[4], (DEPTH, DEC_BATCH, CONV_W - 1, F), F32),
        'norm_mix_g': 1.0 + 0.02 * n(ks[5], (DEPTH, D), F32),
        'pool_w': n(ks[6], (N_POOL, POOL_GROUPS, POOL_GW, POOL_GW), F32) * POOL_GW ** -0.5,
        'pool_scale': 1.0 + 0.1 * n(ks[7], (N_POOL, D), F32),
        'hgrn_w_in': n(ks[8], (N_HGRN, D, 4 * D), F32) * D ** -0.5,
        'hgrn_lb_logits': 0.1 * n(ks[9], (DEPTH, D), F32),
        'hgrn_norm_g': 1.0 + 0.02 * n(ks[10], (N_HGRN, D), F32),
        'hgrn_w_out': n(ks[11], (N_HGRN, D, D), F32) * D ** -0.5,
        'norm_ffn_g': 1.0 + 0.02 * n(ks[12], (DEPTH, D), F32),
        'ffn_w_up': n(ks[13], (DEPTH, D, 2 * F), F32) * D ** -0.5,
        'ffn_conv_w': n(ks[14], (DEPTH, CONV_W, F), F32) * CONV_W ** -0.5,
        'ffn_conv_b': 0.01 * n(ks[15], (DEPTH, F), F32),
        'ffn_w_down': n(ks[16], (DEPTH, F, D), F32) * F ** -0.5,
        'norm_out_g': 1.0 + 0.02 * n(ks[17], (D,), F32),
    }


def reference(x_prompt, x_sample, state_pool, state_hgrn, state_ffn_conv, norm_mix_g, pool_w, pool_scale,
              hgrn_w_in, hgrn_lb_logits, hgrn_norm_g, hgrn_w_out, norm_ffn_g, ffn_w_up, ffn_conv_w,
              ffn_conv_b, ffn_w_down, norm_out_g):
    sm = jax.nn.softmax(hgrn_lb_logits.astype(F32), axis=0)
    lb_all = jnp.cumsum(sm, axis=0) - sm[0]
    weights = (norm_mix_g, pool_w, pool_scale, hgrn_w_in, hgrn_norm_g, hgrn_w_out, norm_ffn_g,
               ffn_w_up, ffn_conv_w, ffn_conv_b, ffn_w_down, norm_out_g)

    Bp = x_prompt.shape[0]
    p_pool = [jnp.zeros((Bp, 0, D_MODEL), x_prompt.dtype) for _ in range(N_POOL)]
    p_hgrn = [jnp.zeros((Bp, HG_HEADS, HG_HEAD_DIM, HG_HEAD_DIM), F32) for _ in range(N_HGRN)]
    p_conv = [jnp.zeros((Bp, CONV_W - 1, D_FF), x_prompt.dtype) for _ in range(DEPTH)]
    y_prompt, pool_p, hgrn_p, conv_p = trunk(x_prompt, p_pool, p_hgrn, p_conv, 0, lb_all, *weights)

    s_pool = [state_pool[j] for j in range(N_POOL)]
    s_hgrn = [state_hgrn[j] for j in range(N_HGRN)]
    s_conv = [state_ffn_conv[i] for i in range(DEPTH)]
    y_sample, pool_s, hgrn_s, conv_s = trunk(x_sample, s_pool, s_hgrn, s_conv, PAST_LEN, lb_all, *weights)

    return (y_prompt, y_sample, pool_p, pool_s, hgrn_p, hgrn_s, conv_p, conv_s)
```

```python
import functools

import jax
import jax.numpy as jnp
from jax import lax
from jax.experimental import pallas as pl
from jax.experimental.pallas import tpu as pltpu

F32 = jnp.float32
BF16 = jnp.bfloat16

EPS = 1e-6
POOL_WINDOWS = (2, 4, 8, 16)
POOL_HIST = 16
CONV_HIST = 8
CONV_W = 3
HEAD_DIM = 128
CHUNK = 64
FFN_CHUNK = 256
VMEM_LIMIT_BYTES = 56 * 1024 * 1024


def _rmsnorm(x, g):
    ms = jnp.mean(x * x, axis=-1, keepdims=True)
    return x * lax.rsqrt(ms + EPS) * g


def _silu(x):
    return x / (1.0 + jnp.exp(-x))


def _const_spec(shape):
    nd = len(shape)
    return pl.BlockSpec(shape, lambda b, j: (0,) * nd, pipeline_mode=pl.Buffered(1))


def _params():
    return pltpu.CompilerParams(
        dimension_semantics=("arbitrary", "arbitrary"),
        vmem_limit_bytes=VMEM_LIMIT_BYTES)


def _pool_kernel(tm, ramp, x_ref, hist_ref, g_ref, w_ref, scale_ref, invw_ref,
                 inv16_ref, out_ref, last_ref, ext_ref, d_ref):
    j = pl.program_id(1)
    gw = w_ref.shape[-1]

    @pl.when(j == 0)
    def _():
        ext_ref[0:POOL_HIST, :] = hist_ref[...]

    x = x_ref[...]
    hn = _rmsnorm(x, g_ref[...])
    ext_ref[POOL_HIST:POOL_HIST + tm, :] = hn

    for gi, win in enumerate(POOL_WINDOWS):
        sl = slice(gi * gw, (gi + 1) * gw)
        ws = hn[:, sl]
        for k in range(1, win):
            ws = ws + ext_ref[POOL_HIST - k:POOL_HIST - k + tm, sl]
        d = ws * invw_ref[:, sl] - hn[:, sl]
        d_ref[:, sl] = d.astype(BF16)
        if ramp:
            @pl.when(j == 0)
            def _(ws=ws, sl=sl):
                d16 = ws[0:POOL_HIST] * inv16_ref[:, sl] - hn[0:POOL_HIST, sl]
                d_ref[0:POOL_HIST, sl] = d16.astype(BF16)

    for gi in range(len(POOL_WINDOWS)):
        sl = slice(gi * gw, (gi + 1) * gw)
        y = jnp.dot(d_ref[:, sl], w_ref[gi], preferred_element_type=F32)
        out_ref[:, sl] = x_ref[:, sl] + y * scale_ref[:, sl]

    tail = ext_ref[tm:tm + POOL_HIST, :]
    last_ref[...] = tail
    ext_ref[0:POOL_HIST, :] = tail


def _pool_layer(x, hist16, g, w_bf16, scale, invw, inv16, *, tm, ramp):
    B, L, D = x.shape
    nt = L // tm
    kern = functools.partial(_pool_kernel, tm, ramp)
    return pl.pallas_call(
        kern,
        grid=(B, nt),
        in_specs=[
            pl.BlockSpec((None, tm, D), lambda b, j: (b, j, 0)),
            pl.BlockSpec((None, POOL_HIST, D), lambda b, j: (b, 0, 0)),
            _const_spec((1, D)),
            _const_spec(w_bf16.shape),
            _const_spec((1, D)),
            _const_spec((1, D)),
            _const_spec(inv16.shape),
        ],
        out_specs=[
            pl.BlockSpec((None, tm, D), lambda b, j: (b, j, 0)),
            pl.BlockSpec((None, POOL_HIST, D), lambda b, j: (b, 0, 0)),
        ],
        out_shape=[
            jax.ShapeDtypeStruct((B, L, D), F32),
            jax.ShapeDtypeStruct((B, POOL_HIST, D), F32),
        ],
        scratch_shapes=[
            pltpu.VMEM((tm + POOL_HIST, D), F32),
            pltpu.VMEM((tm, D), BF16),
        ],
        compiler_params=_params(),
        name="pool_layer",
    )(x, hist16, g, w_bf16, scale, invw, inv16)


def _ffn_kernel(tm, final_norm, x_ref, hist_ref, g_ref, wg_ref, wv_ref, cw_ref,
                cb_ref, wd_ref, gout_ref, out_ref, last_ref, gate_ref):
    j = pl.program_id(1)
    ff = wg_ref.shape[-1]

    @pl.when(j == 0)
    def _():
        gate_ref[0:CONV_HIST, :] = hist_ref[...]

    x = x_ref[...]
    hn = _rmsnorm(x, g_ref[...]).astype(BF16)

    acc = x
    for c0 in range(0, ff, FFN_CHUNK):
        sl = slice(c0, c0 + FFN_CHUNK)
        gate = jnp.dot(hn, wg_ref[:, sl], preferred_element_type=F32)
        val = jnp.dot(hn, wv_ref[:, sl], preferred_element_type=F32)
        gate_ref[CONV_HIST:CONV_HIST + tm, sl] = gate
        conv = cb_ref[:, sl] + cw_ref[CONV_W - 1:CONV_W, sl] * gate
        for k in range(1, CONV_W):
            prev = gate_ref[CONV_HIST - k:CONV_HIST - k + tm, sl]
            conv = conv + cw_ref[CONV_W - 1 - k:CONV_W - k, sl] * prev
        hid = (_silu(conv) * val).astype(BF16)
        acc = acc + jnp.dot(hid, wd_ref[sl, :], preferred_element_type=F32)

    if final_norm:
        acc = _rmsnorm(acc, gout_ref[...])
    out_ref[...] = acc

    tail = gate_ref[tm:tm + CONV_HIST, :]
    last_ref[...] = tail
    gate_ref[0:CONV_HIST, :] = tail


def _ffn_layer(x, hist8, g, wg, wv, cw, cb, wd, gout, *, tm, final_norm):
    B, L, D = x.shape
    ff = wg.shape[-1]
    nt = L // tm
    kern = functools.partial(_ffn_kernel, tm, final_norm)
    return pl.pallas_call(
        kern,
        grid=(B, nt),
        in_specs=[
            pl.BlockSpec((None, tm, D), lambda b, j: (b, j, 0)),
            pl.BlockSpec((None, CONV_HIST, ff), lambda b, j: (b, 0, 0)),
            _const_spec((1, D)),
            _const_spec((D, ff)),
            _const_spec((D, ff)),
            _const_spec((CONV_W, ff)),
            _const_spec((1, ff)),
            _const_spec((ff, D)),
            _const_spec((1, D)),
        ],
        out_specs=[
            pl.BlockSpec((None, tm, D), lambda b, j: (b, j, 0)),
            pl.BlockSpec((None, CONV_HIST, ff), lambda b, j: (b, 0, 0)),
        ],
        out_shape=[
            jax.ShapeDtypeStruct((B, L, D), F32),
            jax.ShapeDtypeStruct((B, CONV_HIST, ff), F32),
        ],
        scratch_shapes=[pltpu.VMEM((tm + CONV_HIST, ff), F32)],
        compiler_params=_params(),
        name="ffn_layer",
    )(x, hist8, g, wg, wv, cw, cb, wd, gout)


def _hgrn_kernel(tm, layer_idx, x_ref, s0_ref, lbl_ref, g_ref, win_ref, gn_ref,
                 wo_ref, out_ref, sfin_ref, st_ref, q_ref, k_ref, v_ref, b_ref,
                 o_ref):
    j = pl.program_id(1)
    nj = pl.num_programs(1)
    D = x_ref.shape[-1]
    nh = D // HEAD_DIM

    @pl.when(j == 0)
    def _():
        for h in range(nh):
            st_ref[h] = s0_ref[h].T

    logits = lbl_ref[...]
    e = jnp.exp(logits - jnp.max(logits, axis=0, keepdims=True))
    lb = jnp.sum(e[1:layer_idx + 1], axis=0, keepdims=True) / jnp.sum(e, axis=0, keepdims=True)

    x = x_ref[...]
    hn = _rmsnorm(x, g_ref[...]).astype(BF16)
    zq = jnp.dot(hn, win_ref[:, 0:D], preferred_element_type=F32)
    zf = jnp.dot(hn, win_ref[:, D:2 * D], preferred_element_type=F32)
    zi = jnp.dot(hn, win_ref[:, 2 * D:3 * D], preferred_element_type=F32)
    q_ref[...] = _silu(zq)
    kk = (1.0 - lb) / (1.0 + jnp.exp(zf))
    k_ref[...] = kk
    v_ref[...] = zi.astype(BF16)
    glog = jnp.log1p(-kk)

    row = lax.broadcasted_iota(jnp.int32, (CHUNK, CHUNK), 0)
    col = lax.broadcasted_iota(jnp.int32, (CHUNK, CHUNK), 1)
    causal = row >= col
    tri = causal.astype(F32)

    for c in range(tm // CHUNK):
        rs = slice(c * CHUNK, (c + 1) * CHUNK)
        b = jnp.dot(tri, glog[rs, :], preferred_element_type=F32,
                    precision=lax.Precision.HIGHEST)
        b_ref[...] = b
        b_mid = b_ref[CHUNK // 2:CHUNK // 2 + 1, :]
        b_end = b_ref[CHUNK - 1:CHUNK, :]
        qc = q_ref[rs, :]
        kc = k_ref[rs, :]
        q_in = (qc * jnp.exp(b)).astype(BF16)
        q_at = (qc * jnp.exp(b - b_mid)).astype(BF16)
        k_at = (kc * jnp.exp(b_mid - b)).astype(BF16)
        k_up = (kc * jnp.exp(b_end - b)).astype(BF16)
        dec = jnp.exp(b_end)
        for h in range(nh):
            hs = slice(h * HEAD_DIM, (h + 1) * HEAD_DIM)
            st = st_ref[h]
            vh = v_ref[rs, hs]
            o_in = lax.dot_general(q_in[:, hs], st.astype(BF16),
                                   (((1,), (1,)), ((), ())),
                                   preferred_element_type=F32)
            a = lax.dot_general(q_at[:, hs], k_at[:, hs],
                                (((1,), (1,)), ((), ())),
                                preferred_element_type=F32)
            a = jnp.where(causal, a, 0.0).astype(BF16)
            o_ref[rs, hs] = o_in + jnp.dot(a, vh, preferred_element_type=F32)
            upd = lax.dot_general(vh, k_up[:, hs], (((0,), (0,)), ((), ())),
                                  preferred_element_type=F32)
            st_ref[h] = st * dec[:, hs] + upd

    zg = jnp.dot(hn, win_ref[:, 3 * D:4 * D], preferred_element_type=F32)
    gate = _silu(zg)
    for h in range(nh):
        hs = slice(h * HEAD_DIM, (h + 1) * HEAD_DIM)
        oh = o_ref[:, hs]
        ms = jnp.mean(oh * oh, axis=-1, keepdims=True)
        on = oh * lax.rsqrt(ms + EPS) * gn_ref[:, hs] * gate[:, hs]
        o_ref[:, hs] = on
    y = jnp.dot(o_ref[...].astype(BF16), wo_ref[...], preferred_element_type=F32)
    out_ref[...] = x + y

    @pl.when(j == nj - 1)
    def _():
        for h in range(nh):
            sfin_ref[h] = st_ref[h].T


def _hgrn_layer(x, s0, lb_logits, g, w_in, gn, w_out, *, tm, layer_idx):
    B, L, D = x.shape
    nh = D // HEAD_DIM
    nt = L // tm
    kern = functools.partial(_hgrn_kernel, tm, layer_idx)
    st_spec = pl.BlockSpec((None, nh, HEAD_DIM, HEAD_DIM), lambda b, j: (b, 0, 0, 0))
    return pl.pallas_call(
        kern,
        grid=(B, nt),
        in_specs=[
            pl.BlockSpec((None, tm, D), lambda b, j: (b, j, 0)),
            st_spec,
            _const_spec(lb_logits.shape),
            _const_spec((1, D)),
            _const_spec((D, 4 * D)),
            _const_spec((1, D)),
            _const_spec((D, D)),
        ],
        out_specs=[
            pl.BlockSpec((None, tm, D), lambda b, j: (b, j, 0)),
            st_spec,
        ],
        out_shape=[
            jax.ShapeDtypeStruct((B, L, D), F32),
            jax.ShapeDtypeStruct((B, nh, HEAD_DIM, HEAD_DIM), F32),
        ],
        scratch_shapes=[
            pltpu.VMEM((nh, HEAD_DIM, HEAD_DIM), F32),
            pltpu.VMEM((tm, D), F32),
            pltpu.VMEM((tm, D), F32),
            pltpu.VMEM((tm, D), BF16),
            pltpu.VMEM((CHUNK, D), F32),
            pltpu.VMEM((tm, D), F32),
        ],
        compiler_params=_params(),
        name="hgrn_layer",
    )(x, s0, lb_logits, g, w_in, gn, w_out)


def _pool_tables(d_model):
    gw = d_model // len(POOL_WINDOWS)
    lane_win = jnp.repeat(jnp.asarray(POOL_WINDOWS, F32), gw)[None, :]
    pos = jnp.arange(POOL_HIST, dtype=F32)[:, None] + 1.0
    inv16 = 1.0 / jnp.minimum(lane_win, pos)
    return 1.0 / lane_win, inv16


def _trunk(x, pool_hist, hgrn_s, conv_hist, ramp, w, *, tm_pool, tm_hgrn, tm_ffn):
    depth = w["norm_mix_g"].shape[0]
    D = x.shape[-1]
    ff = w["ffn_conv_b"].shape[-1]
    invw, inv16 = _pool_tables(D)
    new_pool, new_hgrn, new_conv = [], [], []
    for i in range(depth):
        jm = i // 2
        if i % 2 == 0:
            x, last = _pool_layer(x, pool_hist[jm], w["norm_mix_g"][i][None, :],
                                  w["pool_w"][jm], w["pool_scale"][jm][None, :],
                                  invw, inv16, tm=tm_pool, ramp=ramp)
            new_pool.append(last[:, 1:, :])
        else:
            x, s_fin = _hgrn_layer(x, hgrn_s[jm], w["hgrn_lb_logits"],
                                   w["norm_mix_g"][i][None, :], w["hgrn_w_in"][jm],
                                   w["hgrn_norm_g"][jm][None, :], w["hgrn_w_out"][jm],
                                   tm=tm_hgrn, layer_idx=i)
            new_hgrn.append(s_fin)
        x, last = _ffn_layer(x, conv_hist[i], w["norm_ffn_g"][i][None, :],
                             w["ffn_w_gate"][i], w["ffn_w_val"][i], w["ffn_conv_w"][i],
                             w["ffn_conv_b"][i][None, :], w["ffn_w_down"][i],
                             w["norm_out_g"][None, :], tm=tm_ffn,
                             final_norm=(i == depth - 1))
        new_conv.append(last[:, CONV_HIST - (CONV_W - 1):, :])
    return x, jnp.stack(new_pool), jnp.stack(new_hgrn), jnp.stack(new_conv)


def kernel(x_prompt, x_sample, state_pool, state_hgrn, state_ffn_conv, norm_mix_g, pool_w, pool_scale, hgrn_w_in, hgrn_lb_logits, hgrn_norm_g, hgrn_w_out, norm_ffn_g, ffn_w_up, ffn_conv_w, ffn_conv_b, ffn_w_down, norm_out_g):
    depth, D = norm_mix_g.shape
    ff = ffn_conv_b.shape[-1]
    n_pool, n_hgrn = state_pool.shape[0], state_hgrn.shape[0]
    nh = D // HEAD_DIM
    Bp = x_prompt.shape[0]

    w = dict(
        norm_mix_g=norm_mix_g, pool_w=pool_w.astype(BF16), pool_scale=pool_scale,
        hgrn_w_in=hgrn_w_in.astype(BF16), hgrn_lb_logits=hgrn_lb_logits,
        hgrn_norm_g=hgrn_norm_g, hgrn_w_out=hgrn_w_out.astype(BF16),
        norm_ffn_g=norm_ffn_g,
        ffn_w_gate=ffn_w_up[:, :, :ff].astype(BF16),
        ffn_w_val=ffn_w_up[:, :, ff:].astype(BF16),
        ffn_conv_w=ffn_conv_w, ffn_conv_b=ffn_conv_b,
        ffn_w_down=ffn_w_down.astype(BF16), norm_out_g=norm_out_g)

    p_pool = [jnp.zeros((Bp, POOL_HIST, D), F32)] * n_pool
    p_hgrn = [jnp.zeros((Bp, nh, HEAD_DIM, HEAD_DIM), F32)] * n_hgrn
    p_conv = [jnp.zeros((Bp, CONV_HIST, ff), F32)] * depth
    y_p, pool_p, hgrn_p, conv_p = _trunk(
        x_prompt, p_pool, p_hgrn, p_conv, True, w,
        tm_pool=512, tm_hgrn=256, tm_ffn=512)

    s_pool = [jnp.pad(state_pool[j], ((0, 0), (POOL_HIST - state_pool.shape[2], 0), (0, 0)))
              for j in range(n_pool)]
    s_hgrn = [state_hgrn[j] for j in range(n_hgrn)]
    s_conv = [jnp.pad(state_ffn_conv[i], ((0, 0), (CONV_HIST - (CONV_W - 1), 0), (0, 0)))
              for i in range(depth)]
    Ls = x_sample.shape[1]
    y_s, pool_s, hgrn_s, conv_s = _trunk(
        x_sample, s_pool, s_hgrn, s_conv, False, w,
        tm_pool=Ls, tm_hgrn=Ls, tm_ffn=Ls)

    return (y_p, y_s, pool_p, pool_s, hgrn_p, hgrn_s, conv_p, conv_s)
```

```python
import functools

import jax
import jax.numpy as jnp
from jax import lax
from jax.experimental import pallas as pl
from jax.experimental.pallas import tpu as pltpu

F32 = jnp.float32
BF16 = jnp.bfloat16

EPS = 1e-6
POOL_WINDOWS = (2, 4, 8, 16)
POOL_HIST = 16
CONV_HIST = 8
CONV_W = 3
HEAD_DIM = 128
CHUNK = 64
EXP2_CLAMP = 120.0
MXU_WIDTH = 256
FFN_CHUNK = MXU_WIDTH
VMEM_LIMIT_BYTES = 56 * 1024 * 1024


def _rmsnorm(x, g):
    ms = jnp.mean(x * x, axis=-1, keepdims=True)
    return x * lax.rsqrt(ms + EPS) * g


def _sigmoid(x):
    return 0.5 + 0.5 * jnp.tanh(0.5 * x)


def _silu(x):
    h = 0.5 * x
    return h + h * jnp.tanh(h)


def _dot_cols(lhs, w_ref, c0, c1):
    return jnp.concatenate(
        [jnp.dot(lhs, w_ref[:, c:c + MXU_WIDTH], preferred_element_type=F32)
         for c in range(c0, c1, MXU_WIDTH)], axis=1)


def _const_spec(shape):
    nd = len(shape)
    return pl.BlockSpec(shape, lambda b, j: (0,) * nd, pipeline_mode=pl.Buffered(1))


def _params():
    return pltpu.CompilerParams(
        dimension_semantics=("arbitrary", "arbitrary"),
        vmem_limit_bytes=VMEM_LIMIT_BYTES)


def _pool_kernel(tm, ramp, x_ref, hist_ref, g_ref, w_ref, scale_ref, invw_ref,
                 inv16_ref, out_ref, last_ref, ext_ref, d_ref):
    j = pl.program_id(1)
    gw = w_ref.shape[-1]

    @pl.when(j == 0)
    def _():
        ext_ref[0:POOL_HIST, :] = hist_ref[...]

    x = x_ref[...]
    hn = _rmsnorm(x, g_ref[...])
    ext_ref[POOL_HIST:POOL_HIST + tm, :] = hn

    for gi, win in enumerate(POOL_WINDOWS):
        sl = slice(gi * gw, (gi + 1) * gw)
        ws = hn[:, sl]
        for k in range(1, win):
            ws = ws + ext_ref[POOL_HIST - k:POOL_HIST - k + tm, sl]
        d = ws * invw_ref[:, sl] - hn[:, sl]
        d_ref[:, sl] = d.astype(BF16)
        if ramp:
            @pl.when(j == 0)
            def _(ws=ws, sl=sl):
                d16 = ws[0:POOL_HIST] * inv16_ref[:, sl] - hn[0:POOL_HIST, sl]
                d_ref[0:POOL_HIST, sl] = d16.astype(BF16)

    for gi in range(len(POOL_WINDOWS)):
        sl = slice(gi * gw, (gi + 1) * gw)
        y = jnp.dot(d_ref[:, sl], w_ref[gi], preferred_element_type=F32)
        out_ref[:, sl] = x_ref[:, sl] + y * scale_ref[:, sl]

    tail = ext_ref[tm:tm + POOL_HIST, :]
    last_ref[...] = tail
    ext_ref[0:POOL_HIST, :] = tail


def _pool_layer(x, hist16, g, w_bf16, scale, invw, inv16, *, tm, ramp):
    B, L, D = x.shape
    nt = L // tm
    kern = functools.partial(_pool_kernel, tm, ramp)
    return pl.pallas_call(
        kern,
        grid=(B, nt),
        in_specs=[
            pl.BlockSpec((None, tm, D), lambda b, j: (b, j, 0)),
            pl.BlockSpec((None, POOL_HIST, D), lambda b, j: (b, 0, 0)),
            _const_spec((1, D)),
            _const_spec(w_bf16.shape),
            _const_spec((1, D)),
            _const_spec((1, D)),
            _const_spec(inv16.shape),
        ],
        out_specs=[
            pl.BlockSpec((None, tm, D), lambda b, j: (b, j, 0)),
            pl.BlockSpec((None, POOL_HIST, D), lambda b, j: (b, 0, 0)),
        ],
        out_shape=[
            jax.ShapeDtypeStruct((B, L, D), F32),
            jax.ShapeDtypeStruct((B, POOL_HIST, D), F32),
        ],
        scratch_shapes=[
            pltpu.VMEM((tm + POOL_HIST, D), F32),
            pltpu.VMEM((tm, D), BF16),
        ],
        compiler_params=_params(),
        name="pool_layer",
    )(x, hist16, g, w_bf16, scale, invw, inv16)


def _ffn_kernel(tm, final_norm, x_ref, hist_ref, g_ref, wg_ref, wv_ref, cw_ref,
                cb_ref, wd_ref, gout_ref, out_ref, last_ref, gate_ref, hid_ref):
    j = pl.program_id(1)
    ff = wg_ref.shape[-1]

    @pl.when(j == 0)
    def _():
        gate_ref[0:CONV_HIST, :] = hist_ref[...]

    x = x_ref[...]
    hn = _rmsnorm(x, g_ref[...]).astype(BF16)

    for c0 in range(0, ff, FFN_CHUNK):
        sl = slice(c0, c0 + FFN_CHUNK)
        gate = jnp.dot(hn, wg_ref[:, sl], preferred_element_type=F32)
        val = jnp.dot(hn, wv_ref[:, sl], preferred_element_type=F32)
        gate_ref[CONV_HIST:CONV_HIST + tm, sl] = gate
        conv = cb_ref[:, sl] + cw_ref[CONV_W - 1:CONV_W, sl] * gate
        for k in range(1, CONV_W):
            prev = gate_ref[CONV_HIST - k:CONV_HIST - k + tm, sl]
            conv = conv + cw_ref[CONV_W - 1 - k:CONV_W - k, sl] * prev
        hid_ref[:, sl] = (_silu(conv) * val).astype(BF16)

    acc = x + _dot_cols(hid_ref[...], wd_ref, 0, x.shape[-1])
    if final_norm:
        acc = _rmsnorm(acc, gout_ref[...])
    out_ref[...] = acc

    tail = gate_ref[tm:tm + CONV_HIST, :]
    last_ref[...] = tail
    gate_ref[0:CONV_HIST, :] = tail


def _ffn_layer(x, hist8, g, wg, wv, cw, cb, wd, gout, *, tm, final_norm):
    B, L, D = x.shape
    ff = wg.shape[-1]
    nt = L // tm
    kern = functools.partial(_ffn_kernel, tm, final_norm)
    return pl.pallas_call(
        kern,
        grid=(B, nt),
        in_specs=[
            pl.BlockSpec((None, tm, D), lambda b, j: (b, j, 0)),
            pl.BlockSpec((None, CONV_HIST, ff), lambda b, j: (b, 0, 0)),
            _const_spec((1, D)),
            _const_spec((D, ff)),
            _const_spec((D, ff)),
            _const_spec((CONV_W, ff)),
            _const_spec((1, ff)),
            _const_spec((ff, D)),
            _const_spec((1, D)),
        ],
        out_specs=[
            pl.BlockSpec((None, tm, D), lambda b, j: (b, j, 0)),
            pl.BlockSpec((None, CONV_HIST, ff), lambda b, j: (b, 0, 0)),
        ],
        out_shape=[
            jax.ShapeDtypeStruct((B, L, D), F32),
            jax.ShapeDtypeStruct((B, CONV_HIST, ff), F32),
        ],
        scratch_shapes=[pltpu.VMEM((tm + CONV_HIST, ff), F32),
                        pltpu.VMEM((tm, ff), BF16)],
        compiler_params=_params(),
        name="ffn_layer",
    )(x, hist8, g, wg, wv, cw, cb, wd, gout)


def _cumsum_rows(g):
    rows, width = g.shape
    sub = lax.broadcasted_iota(jnp.int32, (8, width), 0)
    outs = []
    carry = None
    for r0 in range(0, rows, 8):
        x = g[r0:r0 + 8]
        for sh in (1, 2, 4):
            x = x + jnp.where(sub >= sh, pltpu.roll(x, sh, axis=0), 0.0)
        if carry is not None:
            x = x + carry
        carry = x[7:8]
        outs.append(x)
    return jnp.concatenate(outs, axis=0)


def _block_rel(b, block, ref_fn):
    pieces = []
    for r0 in range(0, b.shape[0], block):
        pieces.append(b[r0:r0 + block] - ref_fn(r0))
    return jnp.concatenate(pieces, axis=0)


def _hgrn_kernel(tm, layer_idx, x_ref, s0_ref, lbl_ref, g_ref, win_ref, gn_ref,
                 wo_ref, out_ref, sfin_ref, st_ref, o_ref):
    j = pl.program_id(1)
    nj = pl.num_programs(1)
    D = x_ref.shape[-1]
    nh = D // HEAD_DIM

    @pl.when(j == 0)
    def _():
        for h in range(nh):
            st_ref[h] = s0_ref[h].T

    logits = lbl_ref[...]
    e = jnp.exp(logits - jnp.max(logits, axis=0, keepdims=True))
    lb = jnp.sum(e[1:layer_idx + 1], axis=0, keepdims=True) / jnp.sum(e, axis=0, keepdims=True)

    x = x_ref[...]
    hn = _rmsnorm(x, g_ref[...]).astype(BF16)
    zq = _dot_cols(hn, win_ref, 0, D)
    zf = _dot_cols(hn, win_ref, D, 2 * D)
    zi = _dot_cols(hn, win_ref, 2 * D, 3 * D)
    q = _silu(zq)
    kk = (1.0 - lb) * _sigmoid(-zf)
    v = zi.astype(BF16)
    glog = jnp.log2(1.0 - kk)

    row = lax.broadcasted_iota(jnp.int32, (CHUNK, CHUNK), 0)
    col = lax.broadcasted_iota(jnp.int32, (CHUNK, CHUNK), 1)
    half, blk = CHUNK // 2, CHUNK // 4
    row_blk = jnp.right_shift(row, blk.bit_length() - 1)
    col_blk = jnp.right_shift(col, blk.bit_length() - 1)
    mask2 = ((row >= half) == (col >= half)) & (row_blk > col_blk)
    mask3 = (row_blk == col_blk) & (row >= col)
    nt_dims = (((1,), (1,)), ((), ()))
    tn_dims = (((0,), (0,)), ((), ()))
    zeros_half = jnp.zeros((half, HEAD_DIM), BF16)
    zeros_blk = jnp.zeros((blk, HEAD_DIM), BF16)

    def scores(c, h):
        rs = slice(c * CHUNK, (c + 1) * CHUNK)
        hs = slice(h * HEAD_DIM, (h + 1) * HEAD_DIM)
        b = _cumsum_rows(glog[rs, hs])
        b_end = b[CHUNK - 1:CHUNK]
        qc = q[rs, hs]
        kc = kk[rs, hs]
        q1 = (qc[half:] * jnp.exp2(b[half:] - b[half:half + 1])).astype(BF16)
        k1 = (kc[:half] * jnp.exp2(b[half:half + 1] - b[:half])).astype(BF16)
        m1 = lax.dot_general(q1, jnp.concatenate([k1, zeros_half], axis=0), nt_dims,
                             preferred_element_type=F32)
        q2, k2 = [], []
        for r0 in (0, half):
            ref = b[r0 + blk:r0 + blk + 1]
            lo, hi = slice(r0, r0 + blk), slice(r0 + blk, r0 + half)
            q2 += [zeros_blk, (qc[hi] * jnp.exp2(b[hi] - ref)).astype(BF16)]
            k2 += [(kc[lo] * jnp.exp2(ref - b[lo])).astype(BF16), zeros_blk]
        m2 = lax.dot_general(jnp.concatenate(q2, axis=0), jnp.concatenate(k2, axis=0),
                             nt_dims, preferred_element_type=F32)
        e3 = _block_rel(b, blk, lambda r0: 0.5 * (b[r0:r0 + 1] + b[r0 + blk - 1:r0 + blk]))
        e3 = jnp.clip(e3, -EXP2_CLAMP, EXP2_CLAMP)
        q3 = (qc * jnp.exp2(e3)).astype(BF16)
        k3 = (kc * jnp.exp2(-e3)).astype(BF16)
        m3 = lax.dot_general(q3, k3, nt_dims, preferred_element_type=F32)
        a = jnp.where(mask2, m2, jnp.where(mask3, m3, 0.0))
        a = jnp.concatenate([a[:half], a[half:] + m1], axis=0)
        q_in = (qc * jnp.exp2(b)).astype(BF16)
        k_up = (kc * jnp.exp2(b_end - b)).astype(BF16)
        upd = lax.dot_general(v[rs, hs], k_up, tn_dims, preferred_element_type=F32)
        return a.astype(BF16), q_in, upd, jnp.exp2(b_end)

    def outputs(c, h, a, q_in, upd, dec):
        rs = slice(c * CHUNK, (c + 1) * CHUNK)
        hs = slice(h * HEAD_DIM, (h + 1) * HEAD_DIM)
        st = st_ref[h]
        o_in = lax.dot_general(q_in, st.astype(BF16), nt_dims, preferred_element_type=F32)
        o_ref[rs, hs] = o_in + jnp.dot(a, v[rs, hs], preferred_element_type=F32)
        st_ref[h] = st * dec + upd

    nc = tm // CHUNK
    pending = [scores(0, h) for h in range(nh)]
    for c in range(nc):
        ready = pending
        pending = []
        for h in range(nh):
            if c + 1 < nc:
                pending.append(scores(c + 1, h))
            outputs(c, h, *ready[h])

    zg = _dot_cols(hn, win_ref, 3 * D, 4 * D)
    gate = _silu(zg)
    for h in range(nh):
        hs = slice(h * HEAD_DIM, (h + 1) * HEAD_DIM)
        oh = o_ref[:, hs]
        ms = jnp.mean(oh * oh, axis=-1, keepdims=True)
        on = oh * lax.rsqrt(ms + EPS) * gn_ref[:, hs] * gate[:, hs]
        o_ref[:, hs] = on
    y = _dot_cols(o_ref[...].astype(BF16), wo_ref, 0, D)
    out_ref[...] = x + y

    @pl.when(j == nj - 1)
    def _():
        for h in range(nh):
            sfin_ref[h] = st_ref[h].T


def _hgrn_layer(x, s0, lb_logits, g, w_in, gn, w_out, *, tm, layer_idx):
    B, L, D = x.shape
    nh = D // HEAD_DIM
    nt = L // tm
    kern = functools.partial(_hgrn_kernel, tm, layer_idx)
    st_spec = pl.BlockSpec((None, nh, HEAD_DIM, HEAD_DIM), lambda b, j: (b, 0, 0, 0))
    return pl.pallas_call(
        kern,
        grid=(B, nt),
        in_specs=[
            pl.BlockSpec((None, tm, D), lambda b, j: (b, j, 0)),
            st_spec,
            _const_spec(lb_logits.shape),
            _const_spec((1, D)),
            _const_spec((D, 4 * D)),
            _const_spec((1, D)),
            _const_spec((D, D)),
        ],
        out_specs=[
            pl.BlockSpec((None, tm, D), lambda b, j: (b, j, 0)),
            st_spec,
        ],
        out_shape=[
            jax.ShapeDtypeStruct((B, L, D), F32),
            jax.ShapeDtypeStruct((B, nh, HEAD_DIM, HEAD_DIM), F32),
        ],
        scratch_shapes=[
            pltpu.VMEM((nh, HEAD_DIM, HEAD_DIM), F32),
            pltpu.VMEM((tm, D), F32),
        ],
        compiler_params=_params(),
        name="hgrn_layer",
    )(x, s0, lb_logits, g, w_in, gn, w_out)


def _pool_tables(d_model):
    gw = d_model // len(POOL_WINDOWS)
    lane_win = jnp.repeat(jnp.asarray(POOL_WINDOWS, F32), gw)[None, :]
    pos = jnp.arange(POOL_HIST, dtype=F32)[:, None] + 1.0
    inv16 = 1.0 / jnp.minimum(lane_win, pos)
    return 1.0 / lane_win, inv16


def _trunk(x, pool_hist, hgrn_s, conv_hist, ramp, w, *, tm_pool, tm_hgrn, tm_ffn):
    depth = w["norm_mix_g"].shape[0]
    D = x.shape[-1]
    ff = w["ffn_conv_b"].shape[-1]
    invw, inv16 = _pool_tables(D)
    new_pool, new_hgrn, new_conv = [], [], []
    for i in range(depth):
        jm = i // 2
        if i % 2 == 0:
            x, last = _pool_layer(x, pool_hist[jm], w["norm_mix_g"][i][None, :],
                                  w["pool_w"][jm], w["pool_scale"][jm][None, :],
                                  invw, inv16, tm=tm_pool, ramp=ramp)
            new_pool.append(last[:, 1:, :])
        else:
            x, s_fin = _hgrn_layer(x, hgrn_s[jm], w["hgrn_lb_logits"],
                                   w["norm_mix_g"][i][None, :], w["hgrn_w_in"][jm],
                                   w["hgrn_norm_g"][jm][None, :], w["hgrn_w_out"][jm],
                                   tm=tm_hgrn, layer_idx=i)
            new_hgrn.append(s_fin)
        x, last = _ffn_layer(x, conv_hist[i], w["norm_ffn_g"][i][None, :],
                             w["ffn_w_gate"][i], w["ffn_w_val"][i], w["ffn_conv_w"][i],
                             w["ffn_conv_b"][i][None, :], w["ffn_w_down"][i],
                             w["norm_out_g"][None, :], tm=tm_ffn,
                             final_norm=(i == depth - 1))
        new_conv.append(last[:, CONV_HIST - (CONV_W - 1):, :])
    return x, jnp.stack(new_pool), jnp.stack(new_hgrn), jnp.stack(new_conv)


def kernel(x_prompt, x_sample, state_pool, state_hgrn, state_ffn_conv, norm_mix_g, pool_w, pool_scale, hgrn_w_in, hgrn_lb_logits, hgrn_norm_g, hgrn_w_out, norm_ffn_g, ffn_w_up, ffn_conv_w, ffn_conv_b, ffn_w_down, norm_out_g):
    depth, D = norm_mix_g.shape
    ff = ffn_conv_b.shape[-1]
    n_pool, n_hgrn = state_pool.shape[0], state_hgrn.shape[0]
    nh = D // HEAD_DIM
    Bp = x_prompt.shape[0]

    w = dict(
        norm_mix_g=norm_mix_g, pool_w=pool_w.astype(BF16), pool_scale=pool_scale,
        hgrn_w_in=hgrn_w_in.astype(BF16), hgrn_lb_logits=hgrn_lb_logits,
        hgrn_norm_g=hgrn_norm_g, hgrn_w_out=hgrn_w_out.astype(BF16),
        norm_ffn_g=norm_ffn_g,
        ffn_w_gate=ffn_w_up[:, :, :ff].astype(BF16),
        ffn_w_val=ffn_w_up[:, :, ff:].astype(BF16),
        ffn_conv_w=ffn_conv_w, ffn_conv_b=ffn_conv_b,
        ffn_w_down=ffn_w_down.astype(BF16), norm_out_g=norm_out_g)

    p_pool = [jnp.zeros((Bp, POOL_HIST, D), F32)] * n_pool
    p_hgrn = [jnp.zeros((Bp, nh, HEAD_DIM, HEAD_DIM), F32)] * n_hgrn
    p_conv = [jnp.zeros((Bp, CONV_HIST, ff), F32)] * depth
    y_p, pool_p, hgrn_p, conv_p = _trunk(
        x_prompt, p_pool, p_hgrn, p_conv, True, w,
        tm_pool=512, tm_hgrn=256, tm_ffn=512)

    s_pool = [jnp.pad(state_pool[j], ((0, 0), (POOL_HIST - state_pool.shape[2], 0), (0, 0)))
              for j in range(n_pool)]
    s_hgrn = [state_hgrn[j] for j in range(n_hgrn)]
    s_conv = [jnp.pad(state_ffn_conv[i], ((0, 0), (CONV_HIST - (CONV_W - 1), 0), (0, 0)))
              for i in range(depth)]
    Ls = x_sample.shape[1]
    y_s, pool_s, hgrn_s, conv_s = _trunk(
        x_sample, s_pool, s_hgrn, s_conv, False, w,
        tm_pool=Ls, tm_hgrn=Ls, tm_ffn=Ls)

    return (y_p, y_s, pool_p, pool_s, hgrn_p, hgrn_s, conv_p, conv_s)
```

```python
import functools

import jax
import jax.numpy as jnp
from jax import lax
from jax.experimental import pallas as pl
from jax.experimental.pallas import tpu as pltpu

F32 = jnp.float32
BF16 = jnp.bfloat16

EPS = 1e-6
POOL_WINDOWS = (2, 4, 8, 16)
POOL_HIST = 16
CONV_HIST = 8
CONV_W = 3
HEAD_DIM = 128
CHUNK = 64
EXP2_CLAMP = 120.0
MXU_WIDTH = 256
FFN_CHUNK = MXU_WIDTH
FFN_TILE_ROWS = 512
HGRN_TILE_ROWS = 256
VMEM_LIMIT_BYTES = 56 * 1024 * 1024


def _rmsnorm(x, g):
    ms = jnp.mean(x * x, axis=-1, keepdims=True)
    return x * lax.rsqrt(ms + EPS) * g


def _sigmoid(x):
    return 0.5 + 0.5 * jnp.tanh(0.5 * x)


def _silu(x):
    h = 0.5 * x
    return h + h * jnp.tanh(h)


def _dot_cols(lhs, w_ref, c0, c1):
    return jnp.concatenate(
        [jnp.dot(lhs, w_ref[:, c:c + MXU_WIDTH], preferred_element_type=F32)
         for c in range(c0, c1, MXU_WIDTH)], axis=1)


def _tile_plan(batch, seq_len, rows):
    tl = min(seq_len, rows)
    assert seq_len % tl == 0 and tl % CHUNK == 0
    nseq = max(1, min(batch, rows // tl))
    while batch % nseq:
        nseq -= 1
    return nseq, tl


def _layer_spec(arr, layer, *tail_block_idx):
    tail = arr.shape[1:]
    idx = tuple(tail_block_idx) + (0,) * (len(tail) - len(tail_block_idx))
    return pl.BlockSpec((None,) + tuple(tail), lambda b, j: (layer,) + idx,
                        pipeline_mode=pl.Buffered(1))


def _state_spec(arr, layer, nseq):
    tail = arr.shape[2:]
    return pl.BlockSpec((None, nseq) + tuple(tail),
                        lambda b, j: (layer, b) + (0,) * len(tail))


def _rows_spec(nseq, tl, d):
    return pl.BlockSpec((nseq, tl, d), lambda b, j: (b, j, 0))


def _params():
    return pltpu.CompilerParams(
        dimension_semantics=("arbitrary", "arbitrary"),
        vmem_limit_bytes=VMEM_LIMIT_BYTES)


def _pool_mix(j, ramp, x, hist_ref, g_ref, w_ref, scale_ref, invw_ref, inv16_ref,
              last_ref, ext_ref, d_ref):
    nseq, tl = ext_ref.shape[0], ext_ref.shape[1] - POOL_HIST
    m, d_model = x.shape
    gw = w_ref.shape[-1]

    @pl.when(j == 0)
    def _():
        ext_ref[:, 0:POOL_HIST, :] = hist_ref[...]

    hn = _rmsnorm(x, g_ref[...])
    ext_ref[:, POOL_HIST:POOL_HIST + tl, :] = hn.reshape(nseq, tl, d_model)

    for gi, win in enumerate(POOL_WINDOWS):
        sl = slice(gi * gw, (gi + 1) * gw)
        ws = hn[:, sl]
        for k in range(1, win):
            ws = ws + ext_ref[:, POOL_HIST - k:POOL_HIST - k + tl, sl].reshape(m, gw)
        d_ref[:, sl] = (ws * invw_ref[:, sl] - hn[:, sl]).astype(BF16)
        if ramp:
            @pl.when(j == 0)
            def _(ws=ws, sl=sl):
                for s in range(nseq):
                    r = slice(s * tl, s * tl + POOL_HIST)
                    d_ref[r, sl] = (ws[r] * inv16_ref[:, sl] - hn[r, sl]).astype(BF16)

    y = jnp.concatenate(
        [jnp.dot(d_ref[:, gi * gw:(gi + 1) * gw], w_ref[gi], preferred_element_type=F32)
         for gi in range(len(POOL_WINDOWS))], axis=1)

    tail = ext_ref[:, tl:tl + POOL_HIST, :]
    last_ref[...] = tail
    ext_ref[:, 0:POOL_HIST, :] = tail
    return x + y * scale_ref[...]


def _conv_ffn(j, x, hist_ref, g_ref, wg_ref, wv_ref, cw_ref, cb_ref, wd_ref,
              last_ref, gate_ref, hid_ref):
    nseq, tl = gate_ref.shape[0], gate_ref.shape[1] - CONV_HIST
    m, d_model = x.shape
    ff = wg_ref.shape[-1]

    @pl.when(j == 0)
    def _():
        gate_ref[:, 0:CONV_HIST, :] = hist_ref[...]

    hn = _rmsnorm(x, g_ref[...]).astype(BF16)

    for c0 in range(0, ff, FFN_CHUNK):
        sl = slice(c0, c0 + FFN_CHUNK)
        gate = jnp.dot(hn, wg_ref[:, sl], preferred_element_type=F32)
        val = jnp.dot(hn, wv_ref[:, sl], preferred_element_type=F32)
        gate_ref[:, CONV_HIST:CONV_HIST + tl, sl] = gate.reshape(nseq, tl, FFN_CHUNK)
        conv = cb_ref[:, sl] + cw_ref[CONV_W - 1:CONV_W, sl] * gate
        for k in range(1, CONV_W):
            prev = gate_ref[:, CONV_HIST - k:CONV_HIST - k + tl, sl].reshape(m, FFN_CHUNK)
            conv = conv + cw_ref[CONV_W - 1 - k:CONV_W - k, sl] * prev
        hid_ref[:, sl] = (_silu(conv) * val).astype(BF16)

    tail = gate_ref[:, tl:tl + CONV_HIST, :]
    last_ref[...] = tail
    gate_ref[:, 0:CONV_HIST, :] = tail
    return x + _dot_cols(hid_ref[...], wd_ref, 0, d_model)


def _rows(x_ref):
    x = x_ref[...]
    return x.reshape(-1, x.shape[-1])


def _store_rows(out_ref, y, final_norm, gout_ref):
    if final_norm:
        y = _rmsnorm(y, gout_ref[...])
    out_ref[...] = y.reshape(out_ref.shape)


def _ffn_kernel(final_norm, x_ref, chist_ref, gffn_ref, wg_ref, wv_ref, cw_ref,
                cb_ref, wd_ref, gout_ref, out_ref, clast_ref, gate_ref, hid_ref):
    j = pl.program_id(1)
    y = _conv_ffn(j, _rows(x_ref), chist_ref, gffn_ref, wg_ref, wv_ref, cw_ref,
                  cb_ref, wd_ref, clast_ref, gate_ref, hid_ref)
    _store_rows(out_ref, y, final_norm, gout_ref)


def _pool_ffn_kernel(ramp, final_norm, x_ref, phist_ref, gmix_ref, pw_ref,
                     pscale_ref, invw_ref, inv16_ref, chist_ref, gffn_ref, wg_ref,
                     wv_ref, cw_ref, cb_ref, wd_ref, gout_ref, out_ref, plast_ref,
                     clast_ref, ext_ref, d_ref, gate_ref, hid_ref):
    j = pl.program_id(1)
    x1 = _pool_mix(j, ramp, _rows(x_ref), phist_ref, gmix_ref, pw_ref, pscale_ref,
                   invw_ref, inv16_ref, plast_ref, ext_ref, d_ref)
    y = _conv_ffn(j, x1, chist_ref, gffn_ref, wg_ref, wv_ref, cw_ref, cb_ref,
                  wd_ref, clast_ref, gate_ref, hid_ref)
    _store_rows(out_ref, y, final_norm, gout_ref)


def _ffn_operands(w, layer, conv_hist, conv_layer, nseq):
    ff = w["ffn_conv_b"].shape[-1]
    args = [conv_hist, w["norm_ffn_g"], w["ffn_w_up"], w["ffn_w_up"], w["ffn_conv_w"],
            w["ffn_conv_b"], w["ffn_w_down"], w["norm_out_g"]]
    d_model = w["norm_out_g"].shape[-1]
    specs = [
        _state_spec(conv_hist, conv_layer, nseq),
        _layer_spec(w["norm_ffn_g"], layer),
        pl.BlockSpec((None, d_model, ff), lambda b, j: (layer, 0, 0),
                     pipeline_mode=pl.Buffered(1)),
        pl.BlockSpec((None, d_model, ff), lambda b, j: (layer, 0, 1),
                     pipeline_mode=pl.Buffered(1)),
        _layer_spec(w["ffn_conv_w"], layer),
        _layer_spec(w["ffn_conv_b"], layer),
        _layer_spec(w["ffn_w_down"], layer),
        _layer_spec(w["norm_out_g"], 0),
    ]
    return args, specs


def _ffn_scratch(nseq, tl, ff):
    return [pltpu.VMEM((nseq, tl + CONV_HIST, ff), F32),
            pltpu.VMEM((nseq * tl, ff), BF16)]


def _ffn_layer(x, conv_hist, conv_layer, w, layer, final_norm):
    B, L, D = x.shape
    ff = w["ffn_conv_b"].shape[-1]
    nseq, tl = _tile_plan(B, L, FFN_TILE_ROWS)
    args, specs = _ffn_operands(w, layer, conv_hist, conv_layer, nseq)
    return pl.pallas_call(
        functools.partial(_ffn_kernel, final_norm),
        grid=(B // nseq, L // tl),
        in_specs=[_rows_spec(nseq, tl, D)] + specs,
        out_specs=[_rows_spec(nseq, tl, D),
                   pl.BlockSpec((nseq, CONV_HIST, ff), lambda b, j: (b, 0, 0))],
        out_shape=[jax.ShapeDtypeStruct((B, L, D), F32),
                   jax.ShapeDtypeStruct((B, CONV_HIST, ff), F32)],
        scratch_shapes=_ffn_scratch(nseq, tl, ff),
        compiler_params=_params(),
        name="ffn_layer",
    )(x, *args)


def _pool_ffn_layer(x, pool_hist, pool_layer, conv_hist, conv_layer, w, layer, jm,
                    ramp, final_norm):
    B, L, D = x.shape
    ff = w["ffn_conv_b"].shape[-1]
    nseq, tl = _tile_plan(B, L, FFN_TILE_ROWS)
    ffn_args, ffn_specs = _ffn_operands(w, layer, conv_hist, conv_layer, nseq)
    args = [pool_hist, w["norm_mix_g"], w["pool_w"], w["pool_scale"], w["pool_invw"],
            w["pool_inv16"]] + ffn_args
    specs = [
        _state_spec(pool_hist, pool_layer, nseq),
        _layer_spec(w["norm_mix_g"], layer),
        _layer_spec(w["pool_w"], jm),
        _layer_spec(w["pool_scale"], jm),
        _layer_spec(w["pool_invw"], 0),
        _layer_spec(w["pool_inv16"], 0),
    ] + ffn_specs
    return pl.pallas_call(
        functools.partial(_pool_ffn_kernel, ramp, final_norm),
        grid=(B // nseq, L // tl),
        in_specs=[_rows_spec(nseq, tl, D)] + specs,
        out_specs=[_rows_spec(nseq, tl, D),
                   pl.BlockSpec((nseq, POOL_HIST, D), lambda b, j: (b, 0, 0)),
                   pl.BlockSpec((nseq, CONV_HIST, ff), lambda b, j: (b, 0, 0))],
        out_shape=[jax.ShapeDtypeStruct((B, L, D), F32),
                   jax.ShapeDtypeStruct((B, POOL_HIST, D), F32),
                   jax.ShapeDtypeStruct((B, CONV_HIST, ff), F32)],
        scratch_shapes=[pltpu.VMEM((nseq, tl + POOL_HIST, D), F32),
                        pltpu.VMEM((nseq * tl, D), BF16)] + _ffn_scratch(nseq, tl, ff),
        compiler_params=_params(),
        name="pool_ffn_layer",
    )(x, *args)


def _cumsum_rows(g):
    rows, width = g.shape
    sub = lax.broadcasted_iota(jnp.int32, (8, width), 0)
    outs = []
    carry = None
    for r0 in range(0, rows, 8):
        x = g[r0:r0 + 8]
        for sh in (1, 2, 4):
            x = x + jnp.where(sub >= sh, pltpu.roll(x, sh, axis=0), 0.0)
        if carry is not None:
            x = x + carry
        carry = x[7:8]
        outs.append(x)
    return jnp.concatenate(outs, axis=0)


def _block_rel(b, block, ref_fn):
    pieces = []
    for r0 in range(0, b.shape[0], block):
        pieces.append(b[r0:r0 + block] - ref_fn(r0))
    return jnp.concatenate(pieces, axis=0)


def _hgrn_kernel(layer_idx, x_ref, s0_ref, lbl_ref, g_ref, win_ref, gn_ref,
                 wo_ref, out_ref, sfin_ref, st_ref, o_ref):
    j = pl.program_id(1)
    nj = pl.num_programs(1)
    nseq, tl, D = x_ref.shape
    nh = D // HEAD_DIM

    @pl.when(j == 0)
    def _():
        for s in range(nseq):
            for h in range(nh):
                st_ref[s, h] = s0_ref[s, h].T

    logits = lbl_ref[...]
    e = jnp.exp(logits - jnp.max(logits, axis=0, keepdims=True))
    lb = jnp.sum(e[1:layer_idx + 1], axis=0, keepdims=True) / jnp.sum(e, axis=0, keepdims=True)

    x = _rows(x_ref)
    hn = _rmsnorm(x, g_ref[...]).astype(BF16)
    zq = _dot_cols(hn, win_ref, 0, D)
    zf = _dot_cols(hn, win_ref, D, 2 * D)
    zi = _dot_cols(hn, win_ref, 2 * D, 3 * D)
    q = _silu(zq)
    kk = (1.0 - lb) * _sigmoid(-zf)
    v = zi.astype(BF16)
    glog = jnp.log2(1.0 - kk)

    row = lax.broadcasted_iota(jnp.int32, (CHUNK, CHUNK), 0)
    col = lax.broadcasted_iota(jnp.int32, (CHUNK, CHUNK), 1)
    half, blk = CHUNK // 2, CHUNK // 4
    row_blk = jnp.right_shift(row, blk.bit_length() - 1)
    col_blk = jnp.right_shift(col, blk.bit_length() - 1)
    mask2 = ((row >= half) == (col >= half)) & (row_blk > col_blk)
    mask3 = (row_blk == col_blk) & (row >= col)
    nt_dims = (((1,), (1,)), ((), ()))
    tn_dims = (((0,), (0,)), ((), ()))
    zeros_half = jnp.zeros((half, HEAD_DIM), BF16)
    zeros_blk = jnp.zeros((blk, HEAD_DIM), BF16)

    def scores(r0, h):
        rs = slice(r0, r0 + CHUNK)
        hs = slice(h * HEAD_DIM, (h + 1) * HEAD_DIM)
        b = _cumsum_rows(glog[rs, hs])
        b_end = b[CHUNK - 1:CHUNK]
        qc = q[rs, hs]
        kc = kk[rs, hs]
        q1 = (qc[half:] * jnp.exp2(b[half:] - b[half:half + 1])).astype(BF16)
        k1 = (kc[:half] * jnp.exp2(b[half:half + 1] - b[:half])).astype(BF16)
        m1 = lax.dot_general(q1, jnp.concatenate([k1, zeros_half], axis=0), nt_dims,
                             preferred_element_type=F32)
        q2, k2 = [], []
        for h0 in (0, half):
            ref = b[h0 + blk:h0 + blk + 1]
            lo, hi = slice(h0, h0 + blk), slice(h0 + blk, h0 + half)
            q2 += [zeros_blk, (qc[hi] * jnp.exp2(b[hi] - ref)).astype(BF16)]
            k2 += [(kc[lo] * jnp.exp2(ref - b[lo])).astype(BF16), zeros_blk]
        m2 = lax.dot_general(jnp.concatenate(q2, axis=0), jnp.concatenate(k2, axis=0),
                             nt_dims, preferred_element_type=F32)
        e3 = _block_rel(b, blk, lambda i: 0.5 * (b[i:i + 1] + b[i + blk - 1:i + blk]))
        e3 = jnp.clip(e3, -EXP2_CLAMP, EXP2_CLAMP)
        q3 = (qc * jnp.exp2(e3)).astype(BF16)
        k3 = (kc * jnp.exp2(-e3)).astype(BF16)
        m3 = lax.dot_general(q3, k3, nt_dims, preferred_element_type=F32)
        a = jnp.where(mask2, m2, jnp.where(mask3, m3, 0.0))
        a = jnp.concatenate([a[:half], a[half:] + m1], axis=0)
        q_in = (qc * jnp.exp2(b)).astype(BF16)
        k_up = (kc * jnp.exp2(b_end - b)).astype(BF16)
        upd = lax.dot_general(v[rs, hs], k_up, tn_dims, preferred_element_type=F32)
        return a.astype(BF16), q_in, upd, jnp.exp2(b_end)

    def outputs(s, r0, h, a, q_in, upd, dec):
        rs = slice(r0, r0 + CHUNK)
        hs = slice(h * HEAD_DIM, (h + 1) * HEAD_DIM)
        st = st_ref[s, h]
        o_in = lax.dot_general(q_in, st.astype(BF16), nt_dims, preferred_element_type=F32)
        o_ref[rs, hs] = o_in + jnp.dot(a, v[rs, hs], preferred_element_type=F32)
        st_ref[s, h] = st * dec + upd

    chunks = [(s, s * tl + c * CHUNK) for s in range(nseq) for c in range(tl // CHUNK)]
    pending = [scores(chunks[0][1], h) for h in range(nh)]
    for i, (s, r0) in enumerate(chunks):
        ready = pending
        pending = []
        for h in range(nh):
            if i + 1 < len(chunks):
                pending.append(scores(chunks[i + 1][1], h))
            outputs(s, r0, h, *ready[h])

    zg = _dot_cols(hn, win_ref, 3 * D, 4 * D)
    gate = _silu(zg)
    for h in range(nh):
        hs = slice(h * HEAD_DIM, (h + 1) * HEAD_DIM)
        oh = o_ref[:, hs]
        ms = jnp.mean(oh * oh, axis=-1, keepdims=True)
        o_ref[:, hs] = oh * lax.rsqrt(ms + EPS) * gn_ref[:, hs] * gate[:, hs]
    y = _dot_cols(o_ref[...].astype(BF16), wo_ref, 0, D)
    out_ref[...] = (x + y).reshape(out_ref.shape)

    @pl.when(j == nj - 1)
    def _():
        for s in range(nseq):
            for h in range(nh):
                sfin_ref[s, h] = st_ref[s, h].T


def _hgrn_layer(x, state, state_layer, w, layer, jm):
    B, L, D = x.shape
    nh = D // HEAD_DIM
    nseq, tl = _tile_plan(B, L, HGRN_TILE_ROWS)
    out_state_spec = pl.BlockSpec((nseq, nh, HEAD_DIM, HEAD_DIM), lambda b, j: (b, 0, 0, 0))
    return pl.pallas_call(
        functools.partial(_hgrn_kernel, layer),
        grid=(B // nseq, L // tl),
        in_specs=[
            _rows_spec(nseq, tl, D),
            _state_spec(state, state_layer, nseq),
            pl.BlockSpec(w["hgrn_lb_logits"].shape, lambda b, j: (0, 0),
                         pipeline_mode=pl.Buffered(1)),
            _layer_spec(w["norm_mix_g"], layer),
            _layer_spec(w["hgrn_w_in"], jm),
            _layer_spec(w["hgrn_norm_g"], jm),
            _layer_spec(w["hgrn_w_out"], jm),
        ],
        out_specs=[_rows_spec(nseq, tl, D), out_state_spec],
        out_shape=[jax.ShapeDtypeStruct((B, L, D), F32),
                   jax.ShapeDtypeStruct((B, nh, HEAD_DIM, HEAD_DIM), F32)],
        scratch_shapes=[pltpu.VMEM((nseq, nh, HEAD_DIM, HEAD_DIM), F32),
                        pltpu.VMEM((nseq * tl, D), F32)],
        compiler_params=_params(),
        name="hgrn_layer",
    )(x, state, w["hgrn_lb_logits"], w["norm_mix_g"], w["hgrn_w_in"], w["hgrn_norm_g"],
      w["hgrn_w_out"])


def _trunk(x, pool_hist, hgrn_state, conv_hist, fresh, w):
    depth = w["norm_mix_g"].shape[0]
    new_pool, new_hgrn, new_conv = [], [], []
    for i in range(depth):
        jm = i // 2
        final = i == depth - 1
        conv_layer = 0 if fresh else i
        if i % 2 == 0:
            x, p_last, c_last = _pool_ffn_layer(
                x, pool_hist, 0 if fresh else jm, conv_hist, conv_layer, w, i, jm,
                ramp=fresh, final_norm=final)
            new_pool.append(p_last[:, 1:, :])
        else:
            x, s_fin = _hgrn_layer(x, hgrn_state, 0 if fresh else jm, w, i, jm)
            new_hgrn.append(s_fin)
            x, c_last = _ffn_layer(x, conv_hist, conv_layer, w, i, final)
        new_conv.append(c_last[:, CONV_HIST - (CONV_W - 1):, :])
    return x, jnp.stack(new_pool), jnp.stack(new_hgrn), jnp.stack(new_conv)


def kernel(x_prompt, x_sample, state_pool, state_hgrn, state_ffn_conv, norm_mix_g, pool_w, pool_scale, hgrn_w_in, hgrn_lb_logits, hgrn_norm_g, hgrn_w_out, norm_ffn_g, ffn_w_up, ffn_conv_w, ffn_conv_b, ffn_w_down, norm_out_g):
    depth, D = norm_mix_g.shape
    ff = ffn_conv_b.shape[-1]
    nh = D // HEAD_DIM
    Bp = x_prompt.shape[0]
    gw = D // len(POOL_WINDOWS)

    lane_win = jnp.repeat(jnp.asarray(POOL_WINDOWS, F32), gw)[None, :]
    pos = jnp.arange(POOL_HIST, dtype=F32)[:, None] + 1.0
    w = dict(
        norm_mix_g=norm_mix_g[:, None, :], norm_ffn_g=norm_ffn_g[:, None, :],
        norm_out_g=norm_out_g[None, None, :],
        pool_w=pool_w.astype(BF16), pool_scale=pool_scale[:, None, :],
        pool_invw=(1.0 / lane_win)[None], pool_inv16=(1.0 / jnp.minimum(lane_win, pos))[None],
        hgrn_w_in=hgrn_w_in.astype(BF16), hgrn_lb_logits=hgrn_lb_logits,
        hgrn_norm_g=hgrn_norm_g[:, None, :], hgrn_w_out=hgrn_w_out.astype(BF16),
        ffn_w_up=ffn_w_up.astype(BF16), ffn_conv_w=ffn_conv_w,
        ffn_conv_b=ffn_conv_b[:, None, :], ffn_w_down=ffn_w_down.astype(BF16))

    y_p, pool_p, hgrn_p, conv_p = _trunk(
        x_prompt,
        jnp.zeros((1, Bp, POOL_HIST, D), F32),
        jnp.zeros((1, Bp, nh, HEAD_DIM, HEAD_DIM), F32),
        jnp.zeros((1, Bp, CONV_HIST, ff), F32),
        True, w)

    pool_pad = POOL_HIST - state_pool.shape[2]
    conv_pad = CONV_HIST - state_ffn_conv.shape[2]
    y_s, pool_s, hgrn_s, conv_s = _trunk(
        x_sample,
        jnp.pad(state_pool, ((0, 0), (0, 0), (pool_pad, 0), (0, 0))),
        state_hgrn,
        jnp.pad(state_ffn_conv, ((0, 0), (0, 0), (conv_pad, 0), (0, 0))),
        False, w)

    return (y_p, y_s, pool_p, pool_s, hgrn_p, hgrn_s, conv_p, conv_s)
```

```python
import functools

import jax
import jax.numpy as jnp
from jax import lax
from jax.experimental import pallas as pl
from jax.experimental.pallas import tpu as pltpu

F32 = jnp.float32
BF16 = jnp.bfloat16

EPS = 1e-6
POOL_WINDOWS = (2, 4, 8, 16)
POOL_HIST = 16
CONV_HIST = 8
CONV_W = 3
HEAD_DIM = 128
CHUNK = 64
EXP2_CLAMP = 120.0
MXU_WIDTH = 256
FFN_CHUNK = MXU_WIDTH
FFN_TILE_ROWS = 512
HGRN_TILE_ROWS = 512
HGRN_PART_ROWS = 256
VMEM_LIMIT_BYTES = 56 * 1024 * 1024


def _rmsnorm(x, g):
    ms = jnp.mean(x * x, axis=-1, keepdims=True)
    return x * lax.rsqrt(ms + EPS) * g


def _sigmoid(x):
    return 0.5 + 0.5 * jnp.tanh(0.5 * x)


def _silu(x):
    h = 0.5 * x
    return h + h * jnp.tanh(h)


def _dot_cols(lhs, w_ref, c0, c1):
    return jnp.concatenate(
        [jnp.dot(lhs, w_ref[:, c:c + MXU_WIDTH], preferred_element_type=F32)
         for c in range(c0, c1, MXU_WIDTH)], axis=1)


def _tile_plan(batch, seq_len, rows):
    tl = min(seq_len, rows)
    assert seq_len % tl == 0 and tl % CHUNK == 0
    nseq = max(1, min(batch, rows // tl))
    while batch % nseq:
        nseq -= 1
    return nseq, tl


def _layer_spec(arr, layer, *tail_block_idx):
    tail = arr.shape[1:]
    idx = tuple(tail_block_idx) + (0,) * (len(tail) - len(tail_block_idx))
    return pl.BlockSpec((None,) + tuple(tail), lambda b, j: (layer,) + idx,
                        pipeline_mode=pl.Buffered(1))


def _state_spec(arr, layer, nseq):
    tail = arr.shape[2:]
    return pl.BlockSpec((None, nseq) + tuple(tail),
                        lambda b, j: (layer, b) + (0,) * len(tail))


def _rows_spec(nseq, tl, d):
    return pl.BlockSpec((nseq, tl, d), lambda b, j: (b, j, 0))


def _params():
    return pltpu.CompilerParams(
        dimension_semantics=("arbitrary", "arbitrary"),
        vmem_limit_bytes=VMEM_LIMIT_BYTES)


def _pool_mix(j, ramp, x, hist_ref, g_ref, w_ref, scale_ref, invw_ref, inv16_ref,
              last_ref, carry_ref, d_ref):
    nseq = carry_ref.shape[0]
    m, d_model = x.shape
    tl = m // nseq
    gw = w_ref.shape[-1]
    assert all(b == 2 * a for a, b in zip((1,) + POOL_WINDOWS, POOL_WINDOWS))
    assert POOL_WINDOWS[-1] <= POOL_HIST

    @pl.when(j == 0)
    def _():
        carry_ref[...] = hist_ref[...]

    hn = _rmsnorm(x, g_ref[...])

    for s in range(nseq):
        rows = slice(s * tl, (s + 1) * tl)
        level = jnp.concatenate([carry_ref[s], hn[rows]], axis=0)
        for gi, win in enumerate(POOL_WINDOWS):
            sl = slice(gi * gw, (gi + 1) * gw)
            level = level + pltpu.roll(level, win // 2, axis=0)
            ws = level[POOL_HIST:, 0:gw]
            d_ref[rows, sl] = (ws * invw_ref[:, sl] - hn[rows, sl]).astype(BF16)
            if ramp:
                @pl.when(j == 0)
                def _(ws=ws, sl=sl, r=slice(s * tl, s * tl + POOL_HIST)):
                    d_ref[r, sl] = (ws[0:POOL_HIST] * inv16_ref[:, sl] - hn[r, sl]).astype(BF16)
            if gi + 1 < len(POOL_WINDOWS):
                level = level[:, gw:]
        tail = hn[(s + 1) * tl - POOL_HIST:(s + 1) * tl]
        last_ref[s] = tail
        carry_ref[s] = tail

    y = jnp.concatenate(
        [jnp.dot(d_ref[:, gi * gw:(gi + 1) * gw], w_ref[gi], preferred_element_type=F32)
         for gi in range(len(POOL_WINDOWS))], axis=1)
    return x + y * scale_ref[...]


def _conv_ffn(j, x, hist_ref, g_ref, wg_ref, wv_ref, cw_ref, cb_ref, wd_ref,
              last_ref, gate_ref, hid_ref):
    nseq, tl = gate_ref.shape[0], gate_ref.shape[1] - CONV_HIST
    m, d_model = x.shape
    ff = wg_ref.shape[-1]

    @pl.when(j == 0)
    def _():
        gate_ref[:, 0:CONV_HIST, :] = hist_ref[...]

    hn = _rmsnorm(x, g_ref[...]).astype(BF16)

    for c0 in range(0, ff, FFN_CHUNK):
        sl = slice(c0, c0 + FFN_CHUNK)
        gate = jnp.dot(hn, wg_ref[:, sl], preferred_element_type=F32)
        val = jnp.dot(hn, wv_ref[:, sl], preferred_element_type=F32)
        gate_ref[:, CONV_HIST:CONV_HIST + tl, sl] = gate.reshape(nseq, tl, FFN_CHUNK)
        conv = cb_ref[:, sl] + cw_ref[CONV_W - 1:CONV_W, sl] * gate
        for k in range(1, CONV_W):
            prev = gate_ref[:, CONV_HIST - k:CONV_HIST - k + tl, sl].reshape(m, FFN_CHUNK)
            conv = conv + cw_ref[CONV_W - 1 - k:CONV_W - k, sl] * prev
        hid_ref[:, sl] = (_silu(conv) * val).astype(BF16)

    tail = gate_ref[:, tl:tl + CONV_HIST, :]
    last_ref[...] = tail
    gate_ref[:, 0:CONV_HIST, :] = tail
    return x + _dot_cols(hid_ref[...], wd_ref, 0, d_model)


def _rows(x_ref):
    x = x_ref[...]
    return x.reshape(-1, x.shape[-1])


def _store_rows(out_ref, y, final_norm, gout_ref):
    if final_norm:
        y = _rmsnorm(y, gout_ref[...])
    out_ref[...] = y.reshape(out_ref.shape)


def _ffn_kernel(final_norm, x_ref, chist_ref, gffn_ref, wg_ref, wv_ref, cw_ref,
                cb_ref, wd_ref, gout_ref, out_ref, clast_ref, gate_ref, hid_ref):
    j = pl.program_id(1)
    y = _conv_ffn(j, _rows(x_ref), chist_ref, gffn_ref, wg_ref, wv_ref, cw_ref,
                  cb_ref, wd_ref, clast_ref, gate_ref, hid_ref)
    _store_rows(out_ref, y, final_norm, gout_ref)


def _pool_ffn_kernel(ramp, final_norm, x_ref, phist_ref, gmix_ref, pw_ref,
                     pscale_ref, invw_ref, inv16_ref, chist_ref, gffn_ref, wg_ref,
                     wv_ref, cw_ref, cb_ref, wd_ref, gout_ref, out_ref, plast_ref,
                     clast_ref, carry_ref, d_ref, gate_ref, hid_ref):
    j = pl.program_id(1)
    x1 = _pool_mix(j, ramp, _rows(x_ref), phist_ref, gmix_ref, pw_ref, pscale_ref,
                   invw_ref, inv16_ref, plast_ref, carry_ref, d_ref)
    y = _conv_ffn(j, x1, chist_ref, gffn_ref, wg_ref, wv_ref, cw_ref, cb_ref,
                  wd_ref, clast_ref, gate_ref, hid_ref)
    _store_rows(out_ref, y, final_norm, gout_ref)


def _ffn_operands(w, layer, conv_hist, conv_layer, nseq):
    ff = w["ffn_conv_b"].shape[-1]
    args = [conv_hist, w["norm_ffn_g"], w["ffn_w_up"], w["ffn_w_up"], w["ffn_conv_w"],
            w["ffn_conv_b"], w["ffn_w_down"], w["norm_out_g"]]
    d_model = w["norm_out_g"].shape[-1]
    specs = [
        _state_spec(conv_hist, conv_layer, nseq),
        _layer_spec(w["norm_ffn_g"], layer),
        pl.BlockSpec((None, d_model, ff), lambda b, j: (layer, 0, 0),
                     pipeline_mode=pl.Buffered(1)),
        pl.BlockSpec((None, d_model, ff), lambda b, j: (layer, 0, 1),
                     pipeline_mode=pl.Buffered(1)),
        _layer_spec(w["ffn_conv_w"], layer),
        _layer_spec(w["ffn_conv_b"], layer),
        _layer_spec(w["ffn_w_down"], layer),
        _layer_spec(w["norm_out_g"], 0),
    ]
    return args, specs


def _ffn_scratch(nseq, tl, ff):
    return [pltpu.VMEM((nseq, tl + CONV_HIST, ff), F32),
            pltpu.VMEM((nseq * tl, ff), BF16)]


def _ffn_layer(x, conv_hist, conv_layer, w, layer, final_norm):
    B, L, D = x.shape
    ff = w["ffn_conv_b"].shape[-1]
    nseq, tl = _tile_plan(B, L, FFN_TILE_ROWS)
    args, specs = _ffn_operands(w, layer, conv_hist, conv_layer, nseq)
    return pl.pallas_call(
        functools.partial(_ffn_kernel, final_norm),
        grid=(B // nseq, L // tl),
        in_specs=[_rows_spec(nseq, tl, D)] + specs,
        out_specs=[_rows_spec(nseq, tl, D),
                   pl.BlockSpec((nseq, CONV_HIST, ff), lambda b, j: (b, 0, 0))],
        out_shape=[jax.ShapeDtypeStruct((B, L, D), F32),
                   jax.ShapeDtypeStruct((B, CONV_HIST, ff), F32)],
        scratch_shapes=_ffn_scratch(nseq, tl, ff),
        compiler_params=_params(),
        name="ffn_layer",
    )(x, *args)


def _pool_ffn_layer(x, pool_hist, pool_layer, conv_hist, conv_layer, w, layer, jm,
                    ramp, final_norm):
    B, L, D = x.shape
    ff = w["ffn_conv_b"].shape[-1]
    nseq, tl = _tile_plan(B, L, FFN_TILE_ROWS)
    ffn_args, ffn_specs = _ffn_operands(w, layer, conv_hist, conv_layer, nseq)
    args = [pool_hist, w["norm_mix_g"], w["pool_w"], w["pool_scale"], w["pool_invw"],
            w["pool_inv16"]] + ffn_args
    specs = [
        _state_spec(pool_hist, pool_layer, nseq),
        _layer_spec(w["norm_mix_g"], layer),
        _layer_spec(w["pool_w"], jm),
        _layer_spec(w["pool_scale"], jm),
        _layer_spec(w["pool_invw"], 0),
        _layer_spec(w["pool_inv16"], 0),
    ] + ffn_specs
    return pl.pallas_call(
        functools.partial(_pool_ffn_kernel, ramp, final_norm),
        grid=(B // nseq, L // tl),
        in_specs=[_rows_spec(nseq, tl, D)] + specs,
        out_specs=[_rows_spec(nseq, tl, D),
                   pl.BlockSpec((nseq, POOL_HIST, D), lambda b, j: (b, 0, 0)),
                   pl.BlockSpec((nseq, CONV_HIST, ff), lambda b, j: (b, 0, 0))],
        out_shape=[jax.ShapeDtypeStruct((B, L, D), F32),
                   jax.ShapeDtypeStruct((B, POOL_HIST, D), F32),
                   jax.ShapeDtypeStruct((B, CONV_HIST, ff), F32)],
        scratch_shapes=[pltpu.VMEM((nseq, POOL_HIST, D), F32),
                        pltpu.VMEM((nseq * tl, D), BF16)] + _ffn_scratch(nseq, tl, ff),
        compiler_params=_params(),
        name="pool_ffn_layer",
    )(x, *args)


def _cumsum_rows(g):
    rows, width = g.shape
    sub = lax.broadcasted_iota(jnp.int32, (8, width), 0)
    outs = []
    carry = None
    for r0 in range(0, rows, 8):
        x = g[r0:r0 + 8]
        for sh in (1, 2, 4):
            x = x + jnp.where(sub >= sh, pltpu.roll(x, sh, axis=0), 0.0)
        if carry is not None:
            x = x + carry
        carry = x[7:8]
        outs.append(x)
    return jnp.concatenate(outs, axis=0)


def _block_rel(b, block, ref_fn):
    pieces = []
    for r0 in range(0, b.shape[0], block):
        pieces.append(b[r0:r0 + block] - ref_fn(r0))
    return jnp.concatenate(pieces, axis=0)


def _hgrn_kernel(layer_idx, x_ref, s0_ref, lbl_ref, g_ref, win_ref, gn_ref,
                 wo_ref, out_ref, sfin_ref, st_ref, o_ref):
    j = pl.program_id(1)
    nj = pl.num_programs(1)
    nseq, tl, D = x_ref.shape
    nh = D // HEAD_DIM

    @pl.when(j == 0)
    def _():
        for s in range(nseq):
            for h in range(nh):
                st_ref[s, h] = s0_ref[s, h].T

    logits = lbl_ref[...]
    e = jnp.exp(logits - jnp.max(logits, axis=0, keepdims=True))
    lb = jnp.sum(e[1:layer_idx + 1], axis=0, keepdims=True) / jnp.sum(e, axis=0, keepdims=True)

    x = _rows(x_ref)
    part_rows = min(x.shape[0], HGRN_PART_ROWS)
    nparts = x.shape[0] // part_rows

    npairs = D // MXU_WIDTH
    heads_per_pair = MXU_WIDTH // HEAD_DIM

    def proj(hn, which, c):
        c0 = which * D + c * MXU_WIDTH
        return jnp.dot(hn, win_ref[:, c0:c0 + MXU_WIDTH], preferred_element_type=F32)

    def new_part(p):
        xp = x[p * part_rows:(p + 1) * part_rows]
        return dict(x=xp, hn=_rmsnorm(xp, g_ref[...]).astype(BF16), pairs={})

    def project_steps(part):
        def q_step(c):
            part["pairs"][c] = dict(q=_silu(proj(part["hn"], 0, c)))

        def f_step(c):
            one_minus_lb = 1.0 - lb[:, c * MXU_WIDTH:(c + 1) * MXU_WIDTH]
            kk = one_minus_lb * _sigmoid(-proj(part["hn"], 1, c))
            part["pairs"][c].update(kk=kk, glog=jnp.log2(1.0 - kk))

        def v_step(c):
            part["pairs"][c]["v"] = proj(part["hn"], 2, c).astype(BF16)

        return [functools.partial(step, c) for c in range(npairs)
                for step in (q_step, f_step, v_step)]

    row = lax.broadcasted_iota(jnp.int32, (CHUNK, CHUNK), 0)
    col = lax.broadcasted_iota(jnp.int32, (CHUNK, CHUNK), 1)
    half, blk = CHUNK // 2, CHUNK // 4
    row_blk = jnp.right_shift(row, blk.bit_length() - 1)
    col_blk = jnp.right_shift(col, blk.bit_length() - 1)
    mask2 = ((row >= half) == (col >= half)) & (row_blk > col_blk)
    mask3 = (row_blk == col_blk) & (row >= col)
    nt_dims = (((1,), (1,)), ((), ()))
    tn_dims = (((0,), (0,)), ((), ()))
    zeros_half = jnp.zeros((half, HEAD_DIM), BF16)
    zeros_blk = jnp.zeros((blk, HEAD_DIM), BF16)

    def scores(part, r0, h):
        rs = slice(r0 % part_rows, r0 % part_rows + CHUNK)
        pair = part["pairs"][h // heads_per_pair]
        hs = slice(h % heads_per_pair * HEAD_DIM, (h % heads_per_pair + 1) * HEAD_DIM)
        b = _cumsum_rows(pair["glog"][rs, hs])
        b_end = b[CHUNK - 1:CHUNK]
        qc = pair["q"][rs, hs]
        kc = pair["kk"][rs, hs]
        vc = pair["v"][rs, hs]
        q1 = (qc[half:] * jnp.exp2(b[half:] - b[half:half + 1])).astype(BF16)
        k1 = (kc[:half] * jnp.exp2(b[half:half + 1] - b[:half])).astype(BF16)
        m1 = lax.dot_general(q1, jnp.concatenate([k1, zeros_half], axis=0), nt_dims,
                             preferred_element_type=F32)
        q2, k2 = [], []
        for h0 in (0, half):
            ref = b[h0 + blk:h0 + blk + 1]
            lo, hi = slice(h0, h0 + blk), slice(h0 + blk, h0 + half)
            q2 += [zeros_blk, (qc[hi] * jnp.exp2(b[hi] - ref)).astype(BF16)]
            k2 += [(kc[lo] * jnp.exp2(ref - b[lo])).astype(BF16), zeros_blk]
        m2 = lax.dot_general(jnp.concatenate(q2, axis=0), jnp.concatenate(k2, axis=0),
                             nt_dims, preferred_element_type=F32)
        e3 = _block_rel(b, blk, lambda i: 0.5 * (b[i:i + 1] + b[i + blk - 1:i + blk]))
        e3 = jnp.clip(e3, -EXP2_CLAMP, EXP2_CLAMP)
        q3 = (qc * jnp.exp2(e3)).astype(BF16)
        k3 = (kc * jnp.exp2(-e3)).astype(BF16)
        m3 = lax.dot_general(q3, k3, nt_dims, preferred_element_type=F32)
        a = jnp.where(mask2, m2, jnp.where(mask3, m3, 0.0))
        a = jnp.concatenate([a[:half], a[half:] + m1], axis=0)
        q_in = (qc * jnp.exp2(b)).astype(BF16)
        k_up = (kc * jnp.exp2(b_end - b)).astype(BF16)
        upd = lax.dot_general(vc, k_up, tn_dims, preferred_element_type=F32)
        return a.astype(BF16), q_in, vc, upd, jnp.exp2(b_end)

    def outputs(s, r0, h, a, q_in, vc, upd, dec):
        hs = slice(h * HEAD_DIM, (h + 1) * HEAD_DIM)
        st = st_ref[s, h]
        o_in = lax.dot_general(q_in, st.astype(BF16), nt_dims, preferred_element_type=F32)
        o_ref[r0:r0 + CHUNK, hs] = o_in + jnp.dot(a, vc, preferred_element_type=F32)
        st_ref[s, h] = st * dec + upd

    def gate_pair(p, part, c):
        rows = slice(p * part_rows, (p + 1) * part_rows)
        gate = _silu(proj(part["hn"], 3, c))
        for hp in range(heads_per_pair):
            hs = slice(c * MXU_WIDTH + hp * HEAD_DIM, c * MXU_WIDTH + (hp + 1) * HEAD_DIM)
            oh = o_ref[rows, hs]
            ms = jnp.mean(oh * oh, axis=-1, keepdims=True)
            o_ref[rows, hs] = (oh * lax.rsqrt(ms + EPS) * gn_ref[:, hs]
                               * gate[:, hp * HEAD_DIM:(hp + 1) * HEAD_DIM])

    def out_cols(p, part, c):
        rows = slice(p * part_rows, (p + 1) * part_rows)
        cs = slice(c * MXU_WIDTH, (c + 1) * MXU_WIDTH)
        y = part["x"][:, cs] + jnp.dot(o_ref[rows, :].astype(BF16), wo_ref[:, cs],
                                       preferred_element_type=F32)
        if nseq == 1:
            out_ref[0, rows, cs] = y
        else:
            seqs = slice(p * part_rows // tl, (p + 1) * part_rows // tl)
            out_ref[seqs, :, cs] = y.reshape(part_rows // tl, tl, MXU_WIDTH)

    def finish_steps(p, part):
        steps = [functools.partial(gate_pair, p, part, c) for c in range(npairs)]
        return steps + [functools.partial(out_cols, p, part, c) for c in range(npairs)]

    chunks = [(s, s * tl + c * CHUNK) for s in range(nseq) for c in range(tl // CHUNK)]
    parts = {0: new_part(0)}
    for step in project_steps(parts[0]):
        step()
    pending = [scores(parts[0], chunks[0][1], h) for h in range(nh)]
    for i, (s, r0) in enumerate(chunks):
        p = r0 // part_rows
        if r0 % part_rows == 0 and p + 1 < nparts:
            parts[p + 1] = new_part(p + 1)
            for step in project_steps(parts[p + 1]):
                step()
        ready = pending
        pending = []
        for h in range(nh):
            if i + 1 < len(chunks):
                nxt = chunks[i + 1][1]
                pending.append(scores(parts[nxt // part_rows], nxt, h))
            outputs(s, r0, h, *ready[h])
        if (r0 + CHUNK) % part_rows == 0:
            for step in finish_steps(p, parts.pop(p)):
                step()

    @pl.when(j == nj - 1)
    def _():
        for s in range(nseq):
            for h in range(nh):
                sfin_ref[s, h] = st_ref[s, h].T


def _hgrn_layer(x, state, state_layer, w, layer, jm):
    B, L, D = x.shape
    nh = D // HEAD_DIM
    nseq, tl = _tile_plan(B, L, HGRN_TILE_ROWS)
    out_state_spec = pl.BlockSpec((nseq, nh, HEAD_DIM, HEAD_DIM), lambda b, j: (b, 0, 0, 0))
    return pl.pallas_call(
        functools.partial(_hgrn_kernel, layer),
        grid=(B // nseq, L // tl),
        in_specs=[
            _rows_spec(nseq, tl, D),
            _state_spec(state, state_layer, nseq),
            pl.BlockSpec(w["hgrn_lb_logits"].shape, lambda b, j: (0, 0),
                         pipeline_mode=pl.Buffered(1)),
            _layer_spec(w["norm_mix_g"], layer),
            _layer_spec(w["hgrn_w_in"], jm),
            _layer_spec(w["hgrn_norm_g"], jm),
            _layer_spec(w["hgrn_w_out"], jm),
        ],
        out_specs=[_rows_spec(nseq, tl, D), out_state_spec],
        out_shape=[jax.ShapeDtypeStruct((B, L, D), F32),
                   jax.ShapeDtypeStruct((B, nh, HEAD_DIM, HEAD_DIM), F32)],
        scratch_shapes=[pltpu.VMEM((nseq, nh, HEAD_DIM, HEAD_DIM), F32),
                        pltpu.VMEM((nseq * tl, D), F32)],
        compiler_params=_params(),
        name="hgrn_layer",
    )(x, state, w["hgrn_lb_logits"], w["norm_mix_g"], w["hgrn_w_in"], w["hgrn_norm_g"],
      w["hgrn_w_out"])


def _trunk(x, pool_hist, hgrn_state, conv_hist, fresh, w):
    depth = w["norm_mix_g"].shape[0]
    new_pool, new_hgrn, new_conv = [], [], []
    for i in range(depth):
        jm = i // 2
        final = i == depth - 1
        conv_layer = 0 if fresh else i
        if i % 2 == 0:
            x, p_last, c_last = _pool_ffn_layer(
                x, pool_hist, 0 if fresh else jm, conv_hist, conv_layer, w, i, jm,
                ramp=fresh, final_norm=final)
            new_pool.append(p_last[:, 1:, :])
        else:
            x, s_fin = _hgrn_layer(x, hgrn_state, 0 if fresh else jm, w, i, jm)
            new_hgrn.append(s_fin)
            x, c_last = _ffn_layer(x, conv_hist, conv_layer, w, i, final)
        new_conv.append(c_last[:, CONV_HIST - (CONV_W - 1):, :])
    return x, jnp.stack(new_pool), jnp.stack(new_hgrn), jnp.stack(new_conv)


def kernel(x_prompt, x_sample, state_pool, state_hgrn, state_ffn_conv, norm_mix_g, pool_w, pool_scale, hgrn_w_in, hgrn_lb_logits, hgrn_norm_g, hgrn_w_out, norm_ffn_g, ffn_w_up, ffn_conv_w, ffn_conv_b, ffn_w_down, norm_out_g):
    depth, D = norm_mix_g.shape
    ff = ffn_conv_b.shape[-1]
    nh = D // HEAD_DIM
    Bp = x_prompt.shape[0]
    gw = D // len(POOL_WINDOWS)

    lane_win = jnp.repeat(jnp.asarray(POOL_WINDOWS, F32), gw)[None, :]
    pos = jnp.arange(POOL_HIST, dtype=F32)[:, None] + 1.0
    w = dict(
        norm_mix_g=norm_mix_g[:, None, :], norm_ffn_g=norm_ffn_g[:, None, :],
        norm_out_g=norm_out_g[None, None, :],
        pool_w=pool_w.astype(BF16), pool_scale=pool_scale[:, None, :],
        pool_invw=(1.0 / lane_win)[None], pool_inv16=(1.0 / jnp.minimum(lane_win, pos))[None],
        hgrn_w_in=hgrn_w_in.astype(BF16), hgrn_lb_logits=hgrn_lb_logits,
        hgrn_norm_g=hgrn_norm_g[:, None, :], hgrn_w_out=hgrn_w_out.astype(BF16),
        ffn_w_up=ffn_w_up.astype(BF16), ffn_conv_w=ffn_conv_w,
        ffn_conv_b=ffn_conv_b[:, None, :], ffn_w_down=ffn_w_down.astype(BF16))

    y_p, pool_p, hgrn_p, conv_p = _trunk(
        x_prompt,
        jnp.zeros((1, Bp, POOL_HIST, D), F32),
        jnp.zeros((1, Bp, nh, HEAD_DIM, HEAD_DIM), F32),
        jnp.zeros((1, Bp, CONV_HIST, ff), F32),
        True, w)

    pool_pad = POOL_HIST - state_pool.shape[2]
    conv_pad = CONV_HIST - state_ffn_conv.shape[2]
    y_s, pool_s, hgrn_s, conv_s = _trunk(
        x_sample,
        jnp.pad(state_pool, ((0, 0), (0, 0), (pool_pad, 0), (0, 0))),
        state_hgrn,
        jnp.pad(state_ffn_conv, ((0, 0), (0, 0), (conv_pad, 0), (0, 0))),
        False, w)

    return (y_p, y_s, pool_p, pool_s, hgrn_p, hgrn_s, conv_p, conv_s)
```

```python
import functools

import jax
import jax.numpy as jnp
from jax import lax
from jax.experimental import pallas as pl
from jax.experimental.pallas import tpu as pltpu

F32 = jnp.float32
BF16 = jnp.bfloat16

EPS = 1e-6
POOL_WINDOWS = (2, 4, 8, 16)
POOL_HIST = 16
CONV_HIST = 8
CONV_W = 3
HEAD_DIM = 128
CHUNK = 64
EXP2_CLAMP = 120.0
MXU_WIDTH = 256
FFN_CHUNK = MXU_WIDTH
FFN_TILE_ROWS = 512
POOL_BLOCK_ROWS = 256
HGRN_TILE_ROWS = 512
HGRN_PART_ROWS = 256
VMEM_LIMIT_BYTES = 56 * 1024 * 1024


def _rmsnorm(x, g):
    ms = jnp.mean(x * x, axis=-1, keepdims=True)
    return x * lax.rsqrt(ms + EPS) * g


def _sigmoid(x):
    return 0.5 + 0.5 * jnp.tanh(0.5 * x)


def _silu(x):
    h = 0.5 * x
    return h + h * jnp.tanh(h)


def _dot_cols(lhs, w_ref, c0, c1):
    return jnp.concatenate(
        [jnp.dot(lhs, w_ref[:, c:c + MXU_WIDTH], preferred_element_type=F32)
         for c in range(c0, c1, MXU_WIDTH)], axis=1)


def _tile_plan(batch, seq_len, rows):
    tl = min(seq_len, rows)
    assert seq_len % tl == 0 and tl % CHUNK == 0
    nseq = max(1, min(batch, rows // tl))
    while batch % nseq:
        nseq -= 1
    return nseq, tl


def _layer_spec(arr, layer, *tail_block_idx):
    tail = arr.shape[1:]
    idx = tuple(tail_block_idx) + (0,) * (len(tail) - len(tail_block_idx))
    return pl.BlockSpec((None,) + tuple(tail), lambda b, j: (layer,) + idx,
                        pipeline_mode=pl.Buffered(1))


def _state_spec(arr, layer, nseq):
    tail = arr.shape[2:]
    return pl.BlockSpec((None, nseq) + tuple(tail),
                        lambda b, j: (layer, b) + (0,) * len(tail))


def _rows_spec(nseq, tl, d):
    return pl.BlockSpec((nseq, tl, d), lambda b, j: (b, j, 0))


def _params():
    return pltpu.CompilerParams(
        dimension_semantics=("arbitrary", "arbitrary"),
        vmem_limit_bytes=VMEM_LIMIT_BYTES)


def _pool_steps(x_ref, dst_ref, g_ref, w_ref, scale_ref, invw_ref, inv16_ref, last_ref,
                carry_ref, hn_ref, *, ramp, commit):
    nseq, tl, d_model = x_ref.shape
    gw = w_ref.shape[-1]
    assert all(b == 2 * a for a, b in zip((1,) + POOL_WINDOWS, POOL_WINDOWS))
    assert POOL_WINDOWS[-1] <= POOL_HIST

    seg = min(tl, POOL_BLOCK_ROWS)
    segments = [(s, r0, seg) for s in range(nseq) for r0 in range(0, tl, seg)]
    per_block = max(1, POOL_BLOCK_ROWS // seg)
    blocks = [segments[i:i + per_block] for i in range(0, len(segments), per_block)]

    def norm_step(block):
        for s, r0, n in block:
            rows = slice(s * tl + r0, s * tl + r0 + n)
            hn_ref[rows, :] = _rmsnorm(x_ref[s, r0:r0 + n, :], g_ref[...])

    def group_step(gi, block):
        sl = slice(gi * gw, (gi + 1) * gw)
        for s, r0, n in block:
            first = s * tl + r0
            rows = slice(first, first + n)
            hn = hn_ref[rows, sl]
            prev = carry_ref[s, :, sl] if r0 == 0 else hn_ref[first - POOL_HIST:first, sl]
            level = jnp.concatenate([prev, hn], axis=0)
            for k in range(gi + 1):
                level = level + pltpu.roll(level, 2 ** k, axis=0)
            ws = level[POOL_HIST:]
            d = ws * invw_ref[:, sl] - hn
            if ramp and r0 == 0:
                head = ws[0:POOL_HIST] * inv16_ref[:, sl] - hn[0:POOL_HIST]
                d = jnp.concatenate([head, d[POOL_HIST:]], axis=0)
            y = jnp.dot(d.astype(BF16), w_ref[gi], preferred_element_type=F32)
            dst_ref[rows, sl] = x_ref[s, r0:r0 + n, sl] + y * scale_ref[:, sl]

    def advance():
        for s in range(nseq):
            tail = hn_ref[(s + 1) * tl - POOL_HIST:(s + 1) * tl, :]
            last_ref[s] = tail
            carry_ref[s] = tail

    def commit_step():
        if commit is None:
            advance()
        else:
            pl.when(commit)(advance)

    return ([functools.partial(norm_step, blk) for blk in blocks]
            + [functools.partial(group_step, gi, blk) for blk in blocks
               for gi in range(len(POOL_WINDOWS))]
            + [commit_step])


def _conv_ffn(j, x, hist_ref, g_ref, wg_ref, wv_ref, cw_ref, cb_ref, wd_ref,
              last_ref, gate_ref, hid_ref, side_steps=()):
    side_steps = list(side_steps)
    nseq, tl = gate_ref.shape[0], gate_ref.shape[1] - CONV_HIST
    m, d_model = x.shape
    ff = wg_ref.shape[-1]

    @pl.when(j == 0)
    def _():
        gate_ref[:, 0:CONV_HIST, :] = hist_ref[...]

    hn = _rmsnorm(x, g_ref[...]).astype(BF16)

    for c0 in range(0, ff, FFN_CHUNK):
        sl = slice(c0, c0 + FFN_CHUNK)
        gate = jnp.dot(hn, wg_ref[:, sl], preferred_element_type=F32)
        val = jnp.dot(hn, wv_ref[:, sl], preferred_element_type=F32)
        gate_ref[:, CONV_HIST:CONV_HIST + tl, sl] = gate.reshape(nseq, tl, FFN_CHUNK)
        conv = cb_ref[:, sl] + cw_ref[CONV_W - 1:CONV_W, sl] * gate
        for k in range(1, CONV_W):
            prev = gate_ref[:, CONV_HIST - k:CONV_HIST - k + tl, sl].reshape(m, FFN_CHUNK)
            conv = conv + cw_ref[CONV_W - 1 - k:CONV_W - k, sl] * prev
        hid_ref[:, sl] = (_silu(conv) * val).astype(BF16)

    tail = gate_ref[:, tl:tl + CONV_HIST, :]
    last_ref[...] = tail
    gate_ref[:, 0:CONV_HIST, :] = tail
    hid = hid_ref[...]
    n_blocks = d_model // MXU_WIDTH
    cols = []
    for c in range(n_blocks):
        cols.append(jnp.dot(hid, wd_ref[:, c * MXU_WIDTH:(c + 1) * MXU_WIDTH],
                            preferred_element_type=F32))
        for _ in range(-(-len(side_steps) // (n_blocks - c))):
            side_steps.pop(0)()
    return x + jnp.concatenate(cols, axis=1)


def _rows(x_ref):
    x = x_ref[...]
    return x.reshape(-1, x.shape[-1])


def _store_rows(out_ref, y, final_norm, gout_ref):
    if final_norm:
        y = _rmsnorm(y, gout_ref[...])
    out_ref[...] = y.reshape(out_ref.shape)


def _ffn_kernel(final_norm, x_ref, chist_ref, gffn_ref, wg_ref, wv_ref, cw_ref,
                cb_ref, wd_ref, gout_ref, out_ref, clast_ref, gate_ref, hid_ref):
    j = pl.program_id(1)
    y = _conv_ffn(j, _rows(x_ref), chist_ref, gffn_ref, wg_ref, wv_ref, cw_ref,
                  cb_ref, wd_ref, clast_ref, gate_ref, hid_ref)
    _store_rows(out_ref, y, final_norm, gout_ref)


def _pool_ffn_kernel(ramp, final_norm, xnext_ref, xfirst_ref, phist_ref, gmix_ref,
                     pw_ref, pscale_ref, invw_ref, inv16_ref, chist_ref, gffn_ref,
                     wg_ref, wv_ref, cw_ref, cb_ref, wd_ref, gout_ref, out_ref,
                     plast_ref, clast_ref, carry_ref, hn_ref, mixed_ref, gate_ref, hid_ref):
    j = pl.program_id(1)
    nj = pl.num_programs(1)
    pool = functools.partial(_pool_steps, g_ref=gmix_ref, w_ref=pw_ref, scale_ref=pscale_ref,
                             invw_ref=invw_ref, inv16_ref=inv16_ref, last_ref=plast_ref,
                             carry_ref=carry_ref, hn_ref=hn_ref)

    @pl.when(j == 0)
    def _():
        carry_ref[...] = phist_ref[...]
        for step in pool(xfirst_ref, mixed_ref.at[0], ramp=ramp, commit=None):
            step()

    slot = lax.rem(j, 2)
    pool_next = pool(xnext_ref, mixed_ref.at[1 - slot], ramp=False, commit=j + 1 < nj)
    y = _conv_ffn(j, mixed_ref[slot], chist_ref, gffn_ref, wg_ref, wv_ref, cw_ref, cb_ref,
                  wd_ref, clast_ref, gate_ref, hid_ref, side_steps=pool_next)
    _store_rows(out_ref, y, final_norm, gout_ref)


def _ffn_operands(w, layer, conv_hist, conv_layer, nseq):
    ff = w["ffn_conv_b"].shape[-1]
    args = [conv_hist, w["norm_ffn_g"], w["ffn_w_up"], w["ffn_w_up"], w["ffn_conv_w"],
            w["ffn_conv_b"], w["ffn_w_down"], w["norm_out_g"]]
    d_model = w["norm_out_g"].shape[-1]
    specs = [
        _state_spec(conv_hist, conv_layer, nseq),
        _layer_spec(w["norm_ffn_g"], layer),
        pl.BlockSpec((None, d_model, ff), lambda b, j: (layer, 0, 0),
                     pipeline_mode=pl.Buffered(1)),
        pl.BlockSpec((None, d_model, ff), lambda b, j: (layer, 0, 1),
                     pipeline_mode=pl.Buffered(1)),
        _layer_spec(w["ffn_conv_w"], layer),
        _layer_spec(w["ffn_conv_b"], layer),
        _layer_spec(w["ffn_w_down"], layer),
        _layer_spec(w["norm_out_g"], 0),
    ]
    return args, specs


def _ffn_scratch(nseq, tl, ff):
    return [pltpu.VMEM((nseq, tl + CONV_HIST, ff), F32),
            pltpu.VMEM((nseq * tl, ff), BF16)]


def _ffn_layer(x, conv_hist, conv_layer, w, layer, final_norm):
    B, L, D = x.shape
    ff = w["ffn_conv_b"].shape[-1]
    nseq, tl = _tile_plan(B, L, FFN_TILE_ROWS)
    args, specs = _ffn_operands(w, layer, conv_hist, conv_layer, nseq)
    return pl.pallas_call(
        functools.partial(_ffn_kernel, final_norm),
        grid=(B // nseq, L // tl),
        in_specs=[_rows_spec(nseq, tl, D)] + specs,
        out_specs=[_rows_spec(nseq, tl, D),
                   pl.BlockSpec((nseq, CONV_HIST, ff), lambda b, j: (b, 0, 0))],
        out_shape=[jax.ShapeDtypeStruct((B, L, D), F32),
                   jax.ShapeDtypeStruct((B, CONV_HIST, ff), F32)],
        scratch_shapes=_ffn_scratch(nseq, tl, ff),
        compiler_params=_params(),
        name="ffn_layer",
    )(x, *args)


def _pool_ffn_layer(x, pool_hist, pool_layer, conv_hist, conv_layer, w, layer, jm,
                    ramp, final_norm):
    B, L, D = x.shape
    ff = w["ffn_conv_b"].shape[-1]
    nseq, tl = _tile_plan(B, L, FFN_TILE_ROWS)
    ffn_args, ffn_specs = _ffn_operands(w, layer, conv_hist, conv_layer, nseq)
    args = [pool_hist, w["norm_mix_g"], w["pool_w"], w["pool_scale"], w["pool_invw"],
            w["pool_inv16"]] + ffn_args
    specs = [
        _state_spec(pool_hist, pool_layer, nseq),
        _layer_spec(w["norm_mix_g"], layer),
        _layer_spec(w["pool_w"], jm),
        _layer_spec(w["pool_scale"], jm),
        _layer_spec(w["pool_invw"], 0),
        _layer_spec(w["pool_inv16"], 0),
    ] + ffn_specs
    nt = L // tl
    return pl.pallas_call(
        functools.partial(_pool_ffn_kernel, ramp, final_norm),
        grid=(B // nseq, nt),
        in_specs=[
            pl.BlockSpec((nseq, tl, D), lambda b, j: (b, jnp.minimum(j + 1, nt - 1), 0)),
            pl.BlockSpec((nseq, tl, D), lambda b, j: (b, 0, 0)),
        ] + specs,
        out_specs=[_rows_spec(nseq, tl, D),
                   pl.BlockSpec((nseq, POOL_HIST, D), lambda b, j: (b, 0, 0)),
                   pl.BlockSpec((nseq, CONV_HIST, ff), lambda b, j: (b, 0, 0))],
        out_shape=[jax.ShapeDtypeStruct((B, L, D), F32),
                   jax.ShapeDtypeStruct((B, POOL_HIST, D), F32),
                   jax.ShapeDtypeStruct((B, CONV_HIST, ff), F32)],
        scratch_shapes=[pltpu.VMEM((nseq, POOL_HIST, D), F32),
                        pltpu.VMEM((nseq * tl, D), F32),
                        pltpu.VMEM((2, nseq * tl, D), F32),
                        ] + _ffn_scratch(nseq, tl, ff),
        compiler_params=_params(),
        name="pool_ffn_layer",
    )(x, x, *args)


def _cumsum_rows(g):
    rows, width = g.shape
    sub = lax.broadcasted_iota(jnp.int32, (8, width), 0)
    outs = []
    carry = None
    for r0 in range(0, rows, 8):
        x = g[r0:r0 + 8]
        for sh in (1, 2, 4):
            x = x + jnp.where(sub >= sh, pltpu.roll(x, sh, axis=0), 0.0)
        if carry is not None:
            x = x + carry
        carry = x[7:8]
        outs.append(x)
    return jnp.concatenate(outs, axis=0)


def _block_rel(b, block, ref_fn):
    pieces = []
    for r0 in range(0, b.shape[0], block):
        pieces.append(b[r0:r0 + block] - ref_fn(r0))
    return jnp.concatenate(pieces, axis=0)


def _hgrn_kernel(layer_idx, x_ref, s0_ref, lbl_ref, g_ref, win_ref, gn_ref,
                 wo_ref, out_ref, sfin_ref, st_ref, o_ref):
    j = pl.program_id(1)
    nj = pl.num_programs(1)
    nseq, tl, D = x_ref.shape
    nh = D // HEAD_DIM

    @pl.when(j == 0)
    def _():
        for s in range(nseq):
            for h in range(nh):
                st_ref[s, h] = s0_ref[s, h].T

    logits = lbl_ref[...]
    e = jnp.exp(logits - jnp.max(logits, axis=0, keepdims=True))
    lb = jnp.sum(e[1:layer_idx + 1], axis=0, keepdims=True) / jnp.sum(e, axis=0, keepdims=True)

    x = _rows(x_ref)
    part_rows = min(x.shape[0], HGRN_PART_ROWS)
    nparts = x.shape[0] // part_rows

    npairs = D // MXU_WIDTH
    heads_per_pair = MXU_WIDTH // HEAD_DIM

    def proj(hn, which, c):
        c0 = which * D + c * MXU_WIDTH
        return jnp.dot(hn, win_ref[:, c0:c0 + MXU_WIDTH], preferred_element_type=F32)

    def new_part(p):
        xp = x[p * part_rows:(p + 1) * part_rows]
        return dict(x=xp, hn=_rmsnorm(xp, g_ref[...]).astype(BF16), pairs={})

    def project_steps(part):
        def q_step(c):
            part["pairs"][c] = dict(q=_silu(proj(part["hn"], 0, c)))

        def f_step(c):
            one_minus_lb = 1.0 - lb[:, c * MXU_WIDTH:(c + 1) * MXU_WIDTH]
            kk = one_minus_lb * _sigmoid(-proj(part["hn"], 1, c))
            part["pairs"][c].update(kk=kk, glog=jnp.log2(1.0 - kk))

        def v_step(c):
            part["pairs"][c]["v"] = proj(part["hn"], 2, c).astype(BF16)

        return [functools.partial(step, c) for c in range(npairs)
                for step in (q_step, f_step, v_step)]

    row = lax.broadcasted_iota(jnp.int32, (CHUNK, CHUNK), 0)
    col = lax.broadcasted_iota(jnp.int32, (CHUNK, CHUNK), 1)
    half, blk = CHUNK // 2, CHUNK // 4
    row_blk = jnp.right_shift(row, blk.bit_length() - 1)
    col_blk = jnp.right_shift(col, blk.bit_length() - 1)
    mask2 = ((row >= half) == (col >= half)) & (row_blk > col_blk)
    mask3 = (row_blk == col_blk) & (row >= col)
    nt_dims = (((1,), (1,)), ((), ()))
    tn_dims = (((0,), (0,)), ((), ()))
    zeros_half = jnp.zeros((half, HEAD_DIM), BF16)
    zeros_blk = jnp.zeros((blk, HEAD_DIM), BF16)

    def scores(part, r0, h):
        rs = slice(r0 % part_rows, r0 % part_rows + CHUNK)
        pair = part["pairs"][h // heads_per_pair]
        hs = slice(h % heads_per_pair * HEAD_DIM, (h % heads_per_pair + 1) * HEAD_DIM)
        b = _cumsum_rows(pair["glog"][rs, hs])
        b_end = b[CHUNK - 1:CHUNK]
        qc = pair["q"][rs, hs]
        kc = pair["kk"][rs, hs]
        vc = pair["v"][rs, hs]
        q1 = (qc[half:] * jnp.exp2(b[half:] - b[half:half + 1])).astype(BF16)
        k1 = (kc[:half] * jnp.exp2(b[half:half + 1] - b[:half])).astype(BF16)
        m1 = lax.dot_general(q1, jnp.concatenate([k1, zeros_half], axis=0), nt_dims,
                             preferred_element_type=F32)
        q2, k2 = [], []
        for h0 in (0, half):
            ref = b[h0 + blk:h0 + blk + 1]
            lo, hi = slice(h0, h0 + blk), slice(h0 + blk, h0 + half)
            q2 += [zeros_blk, (qc[hi] * jnp.exp2(b[hi] - ref)).astype(BF16)]
            k2 += [(kc[lo] * jnp.exp2(ref - b[lo])).astype(BF16), zeros_blk]
        m2 = lax.dot_general(jnp.concatenate(q2, axis=0), jnp.concatenate(k2, axis=0),
                             nt_dims, preferred_element_type=F32)
        e3 = _block_rel(b, blk, lambda i: 0.5 * (b[i:i + 1] + b[i + blk - 1:i + blk]))
        e3 = jnp.clip(e3, -EXP2_CLAMP, EXP2_CLAMP)
        q3 = (qc * jnp.exp2(e3)).astype(BF16)
        k3 = (kc * jnp.exp2(-e3)).astype(BF16)
        m3 = lax.dot_general(q3, k3, nt_dims, preferred_element_type=F32)
        a = jnp.where(mask2, m2, jnp.where(mask3, m3, 0.0))
        a = jnp.concatenate([a[:half], a[half:] + m1], axis=0)
        q_in = (qc * jnp.exp2(b)).astype(BF16)
        k_up = (kc * jnp.exp2(b_end - b)).astype(BF16)
        upd = lax.dot_general(vc, k_up, tn_dims, preferred_element_type=F32)
        return a.astype(BF16), q_in, vc, upd, jnp.exp2(b_end)

    def outputs(s, r0, h, a, q_in, vc, upd, dec):
        hs = slice(h * HEAD_DIM, (h + 1) * HEAD_DIM)
        st = st_ref[s, h]
        o_in = lax.dot_general(q_in, st.astype(BF16), nt_dims, preferred_element_type=F32)
        o_ref[r0:r0 + CHUNK, hs] = o_in + jnp.dot(a, vc, preferred_element_type=F32)
        st_ref[s, h] = st * dec + upd

    def gate_pair(p, part, c):
        rows = slice(p * part_rows, (p + 1) * part_rows)
        gate = _silu(proj(part["hn"], 3, c))
        for hp in range(heads_per_pair):
            hs = slice(c * MXU_WIDTH + hp * HEAD_DIM, c * MXU_WIDTH + (hp + 1) * HEAD_DIM)
            oh = o_ref[rows, hs]
            ms = jnp.mean(oh * oh, axis=-1, keepdims=True)
            o_ref[rows, hs] = (oh * lax.rsqrt(ms + EPS) * gn_ref[:, hs]
                               * gate[:, hp * HEAD_DIM:(hp + 1) * HEAD_DIM])

    def out_cols(p, part, c):
        rows = slice(p * part_rows, (p + 1) * part_rows)
        cs = slice(c * MXU_WIDTH, (c + 1) * MXU_WIDTH)
        y = part["x"][:, cs] + jnp.dot(o_ref[rows, :].astype(BF16), wo_ref[:, cs],
                                       preferred_element_type=F32)
        if nseq == 1:
            out_ref[0, rows, cs] = y
        else:
            seqs = slice(p * part_rows // tl, (p + 1) * part_rows // tl)
            out_ref[seqs, :, cs] = y.reshape(part_rows // tl, tl, MXU_WIDTH)

    def finish_steps(p, part):
        steps = [functools.partial(gate_pair, p, part, c) for c in range(npairs)]
        return steps + [functools.partial(out_cols, p, part, c) for c in range(npairs)]

    chunks = [(s, s * tl + c * CHUNK) for s in range(nseq) for c in range(tl // CHUNK)]
    parts = {0: new_part(0)}
    for step in project_steps(parts[0]):
        step()
    pending = [scores(parts[0], chunks[0][1], h) for h in range(nh)]
    for i, (s, r0) in enumerate(chunks):
        p = r0 // part_rows
        if r0 % part_rows == 0 and p + 1 < nparts:
            parts[p + 1] = new_part(p + 1)
            for step in project_steps(parts[p + 1]):
                step()
        ready = pending
        pending = []
        for h in range(nh):
            if i + 1 < len(chunks):
                nxt = chunks[i + 1][1]
                pending.append(scores(parts[nxt // part_rows], nxt, h))
            outputs(s, r0, h, *ready[h])
        if (r0 + CHUNK) % part_rows == 0:
            for step in finish_steps(p, parts.pop(p)):
                step()

    @pl.when(j == nj - 1)
    def _():
        for s in range(nseq):
            for h in range(nh):
                sfin_ref[s, h] = st_ref[s, h].T


def _hgrn_layer(x, state, state_layer, w, layer, jm):
    B, L, D = x.shape
    nh = D // HEAD_DIM
    nseq, tl = _tile_plan(B, L, HGRN_TILE_ROWS)
    out_state_spec = pl.BlockSpec((nseq, nh, HEAD_DIM, HEAD_DIM), lambda b, j: (b, 0, 0, 0))
    return pl.pallas_call(
        functools.partial(_hgrn_kernel, layer),
        grid=(B // nseq, L // tl),
        in_specs=[
            _rows_spec(nseq, tl, D),
            _state_spec(state, state_layer, nseq),
            pl.BlockSpec(w["hgrn_lb_logits"].shape, lambda b, j: (0, 0),
                         pipeline_mode=pl.Buffered(1)),
            _layer_spec(w["norm_mix_g"], layer),
            _layer_spec(w["hgrn_w_in"], jm),
            _layer_spec(w["hgrn_norm_g"], jm),
            _layer_spec(w["hgrn_w_out"], jm),
        ],
        out_specs=[_rows_spec(nseq, tl, D), out_state_spec],
        out_shape=[jax.ShapeDtypeStruct((B, L, D), F32),
                   jax.ShapeDtypeStruct((B, nh, HEAD_DIM, HEAD_DIM), F32)],
        scratch_shapes=[pltpu.VMEM((nseq, nh, HEAD_DIM, HEAD_DIM), F32),
                        pltpu.VMEM((nseq * tl, D), F32)],
        compiler_params=_params(),
        name="hgrn_layer",
    )(x, state, w["hgrn_lb_logits"], w["norm_mix_g"], w["hgrn_w_in"], w["hgrn_norm_g"],
      w["hgrn_w_out"])


def _trunk(x, pool_hist, hgrn_state, conv_hist, fresh, w):
    depth = w["norm_mix_g"].shape[0]
    new_pool, new_hgrn, new_conv = [], [], []
    for i in range(depth):
        jm = i // 2
        final = i == depth - 1
        conv_layer = 0 if fresh else i
        if i % 2 == 0:
            x, p_last, c_last = _pool_ffn_layer(
                x, pool_hist, 0 if fresh else jm, conv_hist, conv_layer, w, i, jm,
                ramp=fresh, final_norm=final)
            new_pool.append(p_last[:, 1:, :])
        else:
            x, s_fin = _hgrn_layer(x, hgrn_state, 0 if fresh else jm, w, i, jm)
            new_hgrn.append(s_fin)
            x, c_last = _ffn_layer(x, conv_hist, conv_layer, w, i, final)
        new_conv.append(c_last[:, CONV_HIST - (CONV_W - 1):, :])
    return x, jnp.stack(new_pool), jnp.stack(new_hgrn), jnp.stack(new_conv)


def kernel(x_prompt, x_sample, state_pool, state_hgrn, state_ffn_conv, norm_mix_g, pool_w, pool_scale, hgrn_w_in, hgrn_lb_logits, hgrn_norm_g, hgrn_w_out, norm_ffn_g, ffn_w_up, ffn_conv_w, ffn_conv_b, ffn_w_down, norm_out_g):
    depth, D = norm_mix_g.shape
    ff = ffn_conv_b.shape[-1]
    nh = D // HEAD_DIM
    Bp = x_prompt.shape[0]
    gw = D // len(POOL_WINDOWS)

    lane_win = jnp.repeat(jnp.asarray(POOL_WINDOWS, F32), gw)[None, :]
    pos = jnp.arange(POOL_HIST, dtype=F32)[:, None] + 1.0
    w = dict(
        norm_mix_g=norm_mix_g[:, None, :], norm_ffn_g=norm_ffn_g[:, None, :],
        norm_out_g=norm_out_g[None, None, :],
        pool_w=pool_w.astype(BF16), pool_scale=pool_scale[:, None, :],
        pool_invw=(1.0 / lane_win)[None], pool_inv16=(1.0 / jnp.minimum(lane_win, pos))[None],
        hgrn_w_in=hgrn_w_in.astype(BF16), hgrn_lb_logits=hgrn_lb_logits,
        hgrn_norm_g=hgrn_norm_g[:, None, :], hgrn_w_out=hgrn_w_out.astype(BF16),
        ffn_w_up=ffn_w_up.astype(BF16), ffn_conv_w=ffn_conv_w,
        ffn_conv_b=ffn_conv_b[:, None, :], ffn_w_down=ffn_w_down.astype(BF16))

    y_p, pool_p, hgrn_p, conv_p = _trunk(
        x_prompt,
        jnp.zeros((1, Bp, POOL_HIST, D), F32),
        jnp.zeros((1, Bp, nh, HEAD_DIM, HEAD_DIM), F32),
        jnp.zeros((1, Bp, CONV_HIST, ff), F32),
        True, w)

    pool_pad = POOL_HIST - state_pool.shape[2]
    conv_pad = CONV_HIST - state_ffn_conv.shape[2]
    y_s, pool_s, hgrn_s, conv_s = _trunk(
        x_sample,
        jnp.pad(state_pool, ((0, 0), (0, 0), (pool_pad, 0), (0, 0))),
        state_hgrn,
        jnp.pad(state_ffn_conv, ((0, 0), (0, 0), (conv_pad, 0), (0, 0))),
        False, w)

    return (y_p, y_s, pool_p, pool_s, hgrn_p, hgrn_s, conv_p, conv_s)
```

```python
import functools

import jax
import jax.numpy as jnp
from jax import lax
from jax.experimental import pallas as pl
from jax.experimental.pallas import tpu as pltpu

F32 = jnp.float32
BF16 = jnp.bfloat16

EPS = 1e-6
POOL_WINDOWS = (2, 4, 8, 16)
POOL_HIST = 16
CONV_HIST = 8
CONV_W = 3
HEAD_DIM = 128
CHUNK = 64
EXP2_CLAMP = 120.0
MXU_WIDTH = 256
FFN_CHUNK = MXU_WIDTH
FFN_TILE_ROWS = 512
POOL_BLOCK_ROWS = 512
HGRN_TILE_ROWS = 512
HGRN_PART_ROWS = 256
VMEM_LIMIT_BYTES = 56 * 1024 * 1024


def _rmsnorm(x, g):
    ms = jnp.mean(x * x, axis=-1, keepdims=True)
    return x * lax.rsqrt(ms + EPS) * g


def _sigmoid(x):
    return 0.5 + 0.5 * jnp.tanh(0.5 * x)


def _silu(x):
    h = 0.5 * x
    return h + h * jnp.tanh(h)


def _dot_cols(lhs, w_ref, c0, c1):
    return jnp.concatenate(
        [jnp.dot(lhs, w_ref[:, c:c + MXU_WIDTH], preferred_element_type=F32)
         for c in range(c0, c1, MXU_WIDTH)], axis=1)


def _tile_plan(batch, seq_len, rows):
    tl = min(seq_len, rows)
    assert seq_len % tl == 0 and tl % CHUNK == 0
    nseq = max(1, min(batch, rows // tl))
    while batch % nseq:
        nseq -= 1
    return nseq, tl


def _layer_spec(arr, layer, *tail_block_idx):
    tail = arr.shape[1:]
    idx = tuple(tail_block_idx) + (0,) * (len(tail) - len(tail_block_idx))
    return pl.BlockSpec((None,) + tuple(tail), lambda b, j: (layer,) + idx,
                        pipeline_mode=pl.Buffered(1))


def _state_spec(arr, layer, nseq):
    tail = arr.shape[2:]
    return pl.BlockSpec((None, nseq) + tuple(tail),
                        lambda b, j: (layer, b) + (0,) * len(tail))


def _rows_spec(nseq, tl, d):
    return pl.BlockSpec((nseq, tl, d), lambda b, j: (b, j, 0))


def _params():
    return pltpu.CompilerParams(
        dimension_semantics=("arbitrary", "arbitrary"),
        vmem_limit_bytes=VMEM_LIMIT_BYTES)


def _pool_steps(x_ref, dst_ref, g_ref, w_ref, scale_ref, invw_ref, inv16_ref, last_ref,
                carry_ref, hn_ref, *, ramp, commit):
    nseq, tl, d_model = x_ref.shape
    gw = w_ref.shape[-1]
    assert all(b == 2 * a for a, b in zip((1,) + POOL_WINDOWS, POOL_WINDOWS))
    assert POOL_WINDOWS[-1] <= POOL_HIST

    seg = min(tl, POOL_BLOCK_ROWS)
    segments = [(s, r0, seg) for s in range(nseq) for r0 in range(0, tl, seg)]
    per_block = max(1, POOL_BLOCK_ROWS // seg)
    blocks = [segments[i:i + per_block] for i in range(0, len(segments), per_block)]

    def norm_step(block):
        for s, r0, n in block:
            rows = slice(s * tl + r0, s * tl + r0 + n)
            hn_ref[rows, :] = _rmsnorm(x_ref[s, r0:r0 + n, :], g_ref[...])

    def group_step(gi, block):
        sl = slice(gi * gw, (gi + 1) * gw)
        for s, r0, n in block:
            first = s * tl + r0
            rows = slice(first, first + n)
            hn = hn_ref[rows, sl]
            prev = carry_ref[s, :, sl] if r0 == 0 else hn_ref[first - POOL_HIST:first, sl]
            level = jnp.concatenate([prev, hn], axis=0)
            for k in range(gi + 1):
                level = level + pltpu.roll(level, 2 ** k, axis=0)
            ws = level[POOL_HIST:]
            d = ws * invw_ref[:, sl] - hn
            if ramp and r0 == 0:
                head = ws[0:POOL_HIST] * inv16_ref[:, sl] - hn[0:POOL_HIST]
                d = jnp.concatenate([head, d[POOL_HIST:]], axis=0)
            y = jnp.dot(d.astype(BF16), w_ref[gi], preferred_element_type=F32)
            dst_ref[rows, sl] = x_ref[s, r0:r0 + n, sl] + y * scale_ref[:, sl]

    def advance():
        for s in range(nseq):
            tail = hn_ref[(s + 1) * tl - POOL_HIST:(s + 1) * tl, :]
            last_ref[s] = tail
            carry_ref[s] = tail

    def commit_step():
        if commit is None:
            advance()
        else:
            pl.when(commit)(advance)

    return ([functools.partial(norm_step, blk) for blk in blocks]
            + [functools.partial(group_step, gi, blk) for blk in blocks
               for gi in range(len(POOL_WINDOWS))]
            + [commit_step])


def _conv_ffn(j, x, hist_ref, g_ref, wg_ref, wv_ref, cw_ref, cb_ref, wd_ref,
              last_ref, gate_ref, hid_ref, side_steps=()):
    side_steps = list(side_steps)
    nseq, tl = gate_ref.shape[0], gate_ref.shape[1] - CONV_HIST
    m, d_model = x.shape
    ff = wg_ref.shape[-1]

    @pl.when(j == 0)
    def _():
        gate_ref[:, 0:CONV_HIST, :] = hist_ref[...]

    hn = _rmsnorm(x, g_ref[...]).astype(BF16)

    for c0 in range(0, ff, FFN_CHUNK):
        sl = slice(c0, c0 + FFN_CHUNK)
        gate = jnp.dot(hn, wg_ref[:, sl], preferred_element_type=F32)
        val = jnp.dot(hn, wv_ref[:, sl], preferred_element_type=F32)
        gate_ref[:, CONV_HIST:CONV_HIST + tl, sl] = gate.reshape(nseq, tl, FFN_CHUNK)
        conv = cb_ref[:, sl] + cw_ref[CONV_W - 1:CONV_W, sl] * gate
        for k in range(1, CONV_W):
            prev = gate_ref[:, CONV_HIST - k:CONV_HIST - k + tl, sl].reshape(m, FFN_CHUNK)
            conv = conv + cw_ref[CONV_W - 1 - k:CONV_W - k, sl] * prev
        hid_ref[:, sl] = (_silu(conv) * val).astype(BF16)

    tail = gate_ref[:, tl:tl + CONV_HIST, :]
    last_ref[...] = tail
    gate_ref[:, 0:CONV_HIST, :] = tail
    hid = hid_ref[...]
    n_blocks = d_model // MXU_WIDTH
    cols = []
    for c in range(n_blocks):
        cols.append(jnp.dot(hid, wd_ref[:, c * MXU_WIDTH:(c + 1) * MXU_WIDTH],
                            preferred_element_type=F32))
        for _ in range(-(-len(side_steps) // (n_blocks - c))):
            side_steps.pop(0)()
    return x + jnp.concatenate(cols, axis=1)


def _rows(x_ref):
    x = x_ref[...]
    return x.reshape(-1, x.shape[-1])


def _store_rows(out_ref, y, final_norm, gout_ref):
    if final_norm:
        y = _rmsnorm(y, gout_ref[...])
    out_ref[...] = y.reshape(out_ref.shape)


def _ffn_kernel(final_norm, x_ref, chist_ref, gffn_ref, wg_ref, wv_ref, cw_ref,
                cb_ref, wd_ref, gout_ref, out_ref, clast_ref, gate_ref, hid_ref):
    j = pl.program_id(1)
    y = _conv_ffn(j, _rows(x_ref), chist_ref, gffn_ref, wg_ref, wv_ref, cw_ref,
                  cb_ref, wd_ref, clast_ref, gate_ref, hid_ref)
    _store_rows(out_ref, y, final_norm, gout_ref)


def _pool_ffn_kernel(ramp, final_norm, lookahead, xnext_ref, xfirst_ref, phist_ref, gmix_ref,
                     pw_ref, pscale_ref, invw_ref, inv16_ref, chist_ref, gffn_ref,
                     wg_ref, wv_ref, cw_ref, cb_ref, wd_ref, gout_ref, out_ref,
                     plast_ref, clast_ref, carry_ref, hn_ref, mixed_ref, gate_ref, hid_ref):
    j = pl.program_id(1)
    nj = pl.num_programs(1)
    pool = functools.partial(_pool_steps, g_ref=gmix_ref, w_ref=pw_ref, scale_ref=pscale_ref,
                             invw_ref=invw_ref, inv16_ref=inv16_ref, last_ref=plast_ref,
                             carry_ref=carry_ref, hn_ref=hn_ref)

    @pl.when(j == 0)
    def _():
        carry_ref[...] = phist_ref[...]
        for step in pool(xfirst_ref, mixed_ref.at[0], ramp=ramp, commit=None):
            step()

    slot = lax.rem(j, 2)
    pool_next = (pool(xnext_ref, mixed_ref.at[1 - slot], ramp=False, commit=j + 1 < nj)
                 if lookahead else ())
    y = _conv_ffn(j, mixed_ref[slot], chist_ref, gffn_ref, wg_ref, wv_ref, cw_ref, cb_ref,
                  wd_ref, clast_ref, gate_ref, hid_ref, side_steps=pool_next)
    _store_rows(out_ref, y, final_norm, gout_ref)


def _ffn_operands(w, layer, conv_hist, conv_layer, nseq):
    ff = w["ffn_conv_b"].shape[-1]
    args = [conv_hist, w["norm_ffn_g"], w["ffn_w_up"], w["ffn_w_up"], w["ffn_conv_w"],
            w["ffn_conv_b"], w["ffn_w_down"], w["norm_out_g"]]
    d_model = w["norm_out_g"].shape[-1]
    specs = [
        _state_spec(conv_hist, conv_layer, nseq),
        _layer_spec(w["norm_ffn_g"], layer),
        pl.BlockSpec((None, d_model, ff), lambda b, j: (layer, 0, 0),
                     pipeline_mode=pl.Buffered(1)),
        pl.BlockSpec((None, d_model, ff), lambda b, j: (layer, 0, 1),
                     pipeline_mode=pl.Buffered(1)),
        _layer_spec(w["ffn_conv_w"], layer),
        _layer_spec(w["ffn_conv_b"], layer),
        _layer_spec(w["ffn_w_down"], layer),
        _layer_spec(w["norm_out_g"], 0),
    ]
    return args, specs


def _ffn_scratch(nseq, tl, ff):
    return [pltpu.VMEM((nseq, tl + CONV_HIST, ff), F32),
            pltpu.VMEM((nseq * tl, ff), BF16)]


def _ffn_layer(x, conv_hist, conv_layer, w, layer, final_norm):
    B, L, D = x.shape
    ff = w["ffn_conv_b"].shape[-1]
    nseq, tl = _tile_plan(B, L, FFN_TILE_ROWS)
    args, specs = _ffn_operands(w, layer, conv_hist, conv_layer, nseq)
    return pl.pallas_call(
        functools.partial(_ffn_kernel, final_norm),
        grid=(B // nseq, L // tl),
        in_specs=[_rows_spec(nseq, tl, D)] + specs,
        out_specs=[_rows_spec(nseq, tl, D),
                   pl.BlockSpec((nseq, CONV_HIST, ff), lambda b, j: (b, 0, 0))],
        out_shape=[jax.ShapeDtypeStruct((B, L, D), F32),
                   jax.ShapeDtypeStruct((B, CONV_HIST, ff), F32)],
        scratch_shapes=_ffn_scratch(nseq, tl, ff),
        compiler_params=_params(),
        name="ffn_layer",
    )(x, *args)


def _pool_ffn_layer(x, pool_hist, pool_layer, conv_hist, conv_layer, w, layer, jm,
                    ramp, final_norm):
    B, L, D = x.shape
    ff = w["ffn_conv_b"].shape[-1]
    nseq, tl = _tile_plan(B, L, FFN_TILE_ROWS)
    ffn_args, ffn_specs = _ffn_operands(w, layer, conv_hist, conv_layer, nseq)
    args = [pool_hist, w["norm_mix_g"], w["pool_w"], w["pool_scale"], w["pool_invw"],
            w["pool_inv16"]] + ffn_args
    specs = [
        _state_spec(pool_hist, pool_layer, nseq),
        _layer_spec(w["norm_mix_g"], layer),
        _layer_spec(w["pool_w"], jm),
        _layer_spec(w["pool_scale"], jm),
        _layer_spec(w["pool_invw"], 0),
        _layer_spec(w["pool_inv16"], 0),
    ] + ffn_specs
    nt = L // tl
    return pl.pallas_call(
        functools.partial(_pool_ffn_kernel, ramp, final_norm, nt > 1),
        grid=(B // nseq, nt),
        in_specs=[
            pl.BlockSpec((nseq, tl, D), lambda b, j: (b, jnp.minimum(j + 1, nt - 1), 0)),
            pl.BlockSpec((nseq, tl, D), lambda b, j: (b, 0, 0)),
        ] + specs,
        out_specs=[_rows_spec(nseq, tl, D),
                   pl.BlockSpec((nseq, POOL_HIST, D), lambda b, j: (b, 0, 0)),
                   pl.BlockSpec((nseq, CONV_HIST, ff), lambda b, j: (b, 0, 0))],
        out_shape=[jax.ShapeDtypeStruct((B, L, D), F32),
                   jax.ShapeDtypeStruct((B, POOL_HIST, D), F32),
                   jax.ShapeDtypeStruct((B, CONV_HIST, ff), F32)],
        scratch_shapes=[pltpu.VMEM((nseq, POOL_HIST, D), F32),
                        pltpu.VMEM((nseq * tl, D), F32),
                        pltpu.VMEM((2, nseq * tl, D), F32),
                        ] + _ffn_scratch(nseq, tl, ff),
        compiler_params=_params(),
        name="pool_ffn_layer",
    )(x, x, *args)


def _cumsum_rows(g):
    rows, width = g.shape
    sub = lax.broadcasted_iota(jnp.int32, (8, width), 0)
    outs = []
    carry = None
    for r0 in range(0, rows, 8):
        x = g[r0:r0 + 8]
        for sh in (1, 2, 4):
            x = x + jnp.where(sub >= sh, pltpu.roll(x, sh, axis=0), 0.0)
        if carry is not None:
            x = x + carry
        carry = x[7:8]
        outs.append(x)
    return jnp.concatenate(outs, axis=0)


def _block_rel(b, block, ref_fn):
    pieces = []
    for r0 in range(0, b.shape[0], block):
        pieces.append(b[r0:r0 + block] - ref_fn(r0))
    return jnp.concatenate(pieces, axis=0)


def _hgrn_kernel(layer_idx, x_ref, s0_ref, lbl_ref, g_ref, win_ref, gn_ref,
                 wo_ref, out_ref, sfin_ref, st_ref, o_ref):
    j = pl.program_id(1)
    nj = pl.num_programs(1)
    nseq, tl, D = x_ref.shape
    nh = D // HEAD_DIM

    @pl.when(j == 0)
    def _():
        for s in range(nseq):
            for h in range(nh):
                st_ref[s, h] = s0_ref[s, h].T

    logits = lbl_ref[...]
    e = jnp.exp(logits - jnp.max(logits, axis=0, keepdims=True))
    lb = jnp.sum(e[1:layer_idx + 1], axis=0, keepdims=True) / jnp.sum(e, axis=0, keepdims=True)

    x = _rows(x_ref)
    part_rows = min(x.shape[0], HGRN_PART_ROWS)
    nparts = x.shape[0] // part_rows

    npairs = D // MXU_WIDTH
    heads_per_pair = MXU_WIDTH // HEAD_DIM

    def proj(hn, which, c):
        c0 = which * D + c * MXU_WIDTH
        return jnp.dot(hn, win_ref[:, c0:c0 + MXU_WIDTH], preferred_element_type=F32)

    def new_part(p):
        xp = x[p * part_rows:(p + 1) * part_rows]
        return dict(x=xp, hn=_rmsnorm(xp, g_ref[...]).astype(BF16), pairs={})

    def project_steps(part):
        def q_step(c):
            part["pairs"][c] = dict(q=_silu(proj(part["hn"], 0, c)))

        def f_step(c):
            one_minus_lb = 1.0 - lb[:, c * MXU_WIDTH:(c + 1) * MXU_WIDTH]
            kk = one_minus_lb * _sigmoid(-proj(part["hn"], 1, c))
            part["pairs"][c].update(kk=kk, glog=jnp.log2(1.0 - kk))

        def v_step(c):
            part["pairs"][c]["v"] = proj(part["hn"], 2, c).astype(BF16)

        return [functools.partial(step, c) for c in range(npairs)
                for step in (q_step, f_step, v_step)]

    row = lax.broadcasted_iota(jnp.int32, (CHUNK, CHUNK), 0)
    col = lax.broadcasted_iota(jnp.int32, (CHUNK, CHUNK), 1)
    half, blk = CHUNK // 2, CHUNK // 4
    row_blk = jnp.right_shift(row, blk.bit_length() - 1)
    col_blk = jnp.right_shift(col, blk.bit_length() - 1)
    mask2 = ((row >= half) == (col >= half)) & (row_blk > col_blk)
    mask3 = (row_blk == col_blk) & (row >= col)
    nt_dims = (((1,), (1,)), ((), ()))
    tn_dims = (((0,), (0,)), ((), ()))
    zeros_half = jnp.zeros((half, HEAD_DIM), BF16)
    zeros_blk = jnp.zeros((blk, HEAD_DIM), BF16)

    def scores(part, r0, h):
        rs = slice(r0 % part_rows, r0 % part_rows + CHUNK)
        pair = part["pairs"][h // heads_per_pair]
        hs = slice(h % heads_per_pair * HEAD_DIM, (h % heads_per_pair + 1) * HEAD_DIM)
        b = _cumsum_rows(pair["glog"][rs, hs])
        b_end = b[CHUNK - 1:CHUNK]
        qc = pair["q"][rs, hs]
        kc = pair["kk"][rs, hs]
        vc = pair["v"][rs, hs]
        q1 = (qc[half:] * jnp.exp2(b[half:] - b[half:half + 1])).astype(BF16)
        k1 = (kc[:half] * jnp.exp2(b[half:half + 1] - b[:half])).astype(BF16)
        m1 = lax.dot_general(q1, jnp.concatenate([k1, zeros_half], axis=0), nt_dims,
                             preferred_element_type=F32)
        q2, k2 = [], []
        for h0 in (0, half):
            ref = b[h0 + blk:h0 + blk + 1]
            lo, hi = slice(h0, h0 + blk), slice(h0 + blk, h0 + half)
            q2 += [zeros_blk, (qc[hi] * jnp.exp2(b[hi] - ref)).astype(BF16)]
            k2 += [(kc[lo] * jnp.exp2(ref - b[lo])).astype(BF16), zeros_blk]
        m2 = lax.dot_general(jnp.concatenate(q2, axis=0), jnp.concatenate(k2, axis=0),
                             nt_dims, preferred_element_type=F32)
        e3 = _block_rel(b, blk, lambda i: 0.5 * (b[i:i + 1] + b[i + blk - 1:i + blk]))
        e3 = jnp.clip(e3, -EXP2_CLAMP, EXP2_CLAMP)
        q3 = (qc * jnp.exp2(e3)).astype(BF16)
        k3 = (kc * jnp.exp2(-e3)).astype(BF16)
        m3 = lax.dot_general(q3, k3, nt_dims, preferred_element_type=F32)
        a = jnp.where(mask2, m2, jnp.where(mask3, m3, 0.0))
        a = jnp.concatenate([a[:half], a[half:] + m1], axis=0)
        q_in = (qc * jnp.exp2(b)).astype(BF16)
        k_up = (kc * jnp.exp2(b_end - b)).astype(BF16)
        upd = lax.dot_general(vc, k_up, tn_dims, preferred_element_type=F32)
        return a.astype(BF16), q_in, vc, upd, jnp.exp2(b_end)

    def outputs(s, r0, h, a, q_in, vc, upd, dec):
        hs = slice(h * HEAD_DIM, (h + 1) * HEAD_DIM)
        st = st_ref[s, h]
        o_in = lax.dot_general(q_in, st.astype(BF16), nt_dims, preferred_element_type=F32)
        o_ref[r0:r0 + CHUNK, hs] = o_in + jnp.dot(a, vc, preferred_element_type=F32)
        st_ref[s, h] = st * dec + upd

    def gate_pair(p, part, c):
        rows = slice(p * part_rows, (p + 1) * part_rows)
        gate = _silu(proj(part["hn"], 3, c))
        for hp in range(heads_per_pair):
            hs = slice(c * MXU_WIDTH + hp * HEAD_DIM, c * MXU_WIDTH + (hp + 1) * HEAD_DIM)
            oh = o_ref[rows, hs]
            ms = jnp.mean(oh * oh, axis=-1, keepdims=True)
            o_ref[rows, hs] = (oh * lax.rsqrt(ms + EPS) * gn_ref[:, hs]
                               * gate[:, hp * HEAD_DIM:(hp + 1) * HEAD_DIM])

    def out_cols(p, part, c):
        rows = slice(p * part_rows, (p + 1) * part_rows)
        cs = slice(c * MXU_WIDTH, (c + 1) * MXU_WIDTH)
        y = part["x"][:, cs] + jnp.dot(o_ref[rows, :].astype(BF16), wo_ref[:, cs],
                                       preferred_element_type=F32)
        if nseq == 1:
            out_ref[0, rows, cs] = y
        else:
            seqs = slice(p * part_rows // tl, (p + 1) * part_rows // tl)
            out_ref[seqs, :, cs] = y.reshape(part_rows // tl, tl, MXU_WIDTH)

    def finish_steps(p, part):
        steps = [functools.partial(gate_pair, p, part, c) for c in range(npairs)]
        return steps + [functools.partial(out_cols, p, part, c) for c in range(npairs)]

    chunks = [(s, s * tl + c * CHUNK) for s in range(nseq) for c in range(tl // CHUNK)]
    parts = {0: new_part(0)}
    for step in project_steps(parts[0]):
        step()
    pending = [scores(parts[0], chunks[0][1], h) for h in range(nh)]
    for i, (s, r0) in enumerate(chunks):
        p = r0 // part_rows
        if r0 % part_rows == 0 and p + 1 < nparts:
            parts[p + 1] = new_part(p + 1)
            for step in project_steps(parts[p + 1]):
                step()
        ready = pending
        pending = []
        for h in range(nh):
            if i + 1 < len(chunks):
                nxt = chunks[i + 1][1]
                pending.append(scores(parts[nxt // part_rows], nxt, h))
            outputs(s, r0, h, *ready[h])
        if (r0 + CHUNK) % part_rows == 0:
            for step in finish_steps(p, parts.pop(p)):
                step()

    @pl.when(j == nj - 1)
    def _():
        for s in range(nseq):
            for h in range(nh):
                sfin_ref[s, h] = st_ref[s, h].T


def _hgrn_layer(x, state, state_layer, w, layer, jm):
    B, L, D = x.shape
    nh = D // HEAD_DIM
    nseq, tl = _tile_plan(B, L, HGRN_TILE_ROWS)
    out_state_spec = pl.BlockSpec((nseq, nh, HEAD_DIM, HEAD_DIM), lambda b, j: (b, 0, 0, 0))
    return pl.pallas_call(
        functools.partial(_hgrn_kernel, layer),
        grid=(B // nseq, L // tl),
        in_specs=[
            _rows_spec(nseq, tl, D),
            _state_spec(state, state_layer, nseq),
            pl.BlockSpec(w["hgrn_lb_logits"].shape, lambda b, j: (0, 0),
                         pipeline_mode=pl.Buffered(1)),
            _layer_spec(w["norm_mix_g"], layer),
            _layer_spec(w["hgrn_w_in"], jm),
            _layer_spec(w["hgrn_norm_g"], jm),
            _layer_spec(w["hgrn_w_out"], jm),
        ],
        out_specs=[_rows_spec(nseq, tl, D), out_state_spec],
        out_shape=[jax.ShapeDtypeStruct((B, L, D), F32),
                   jax.ShapeDtypeStruct((B, nh, HEAD_DIM, HEAD_DIM), F32)],
        scratch_shapes=[pltpu.VMEM((nseq, nh, HEAD_DIM, HEAD_DIM), F32),
                        pltpu.VMEM((nseq * tl, D), F32)],
        compiler_params=_params(),
        name="hgrn_layer",
    )(x, state, w["hgrn_lb_logits"], w["norm_mix_g"], w["hgrn_w_in"], w["hgrn_norm_g"],
      w["hgrn_w_out"])


def _trunk(x, pool_hist, hgrn_state, conv_hist, fresh, w):
    depth = w["norm_mix_g"].shape[0]
    new_pool, new_hgrn, new_conv = [], [], []
    for i in range(depth):
        jm = i // 2
        final = i == depth - 1
        conv_layer = 0 if fresh else i
        if i % 2 == 0:
            x, p_last, c_last = _pool_ffn_layer(
                x, pool_hist, 0 if fresh else jm, conv_hist, conv_layer, w, i, jm,
                ramp=fresh, final_norm=final)
            new_pool.append(p_last[:, 1:, :])
        else:
            x, s_fin = _hgrn_layer(x, hgrn_state, 0 if fresh else jm, w, i, jm)
            new_hgrn.append(s_fin)
            x, c_last = _ffn_layer(x, conv_hist, conv_layer, w, i, final)
        new_conv.append(c_last[:, CONV_HIST - (CONV_W - 1):, :])
    return x, jnp.stack(new_pool), jnp.stack(new_hgrn), jnp.stack(new_conv)


def kernel(x_prompt, x_sample, state_pool, state_hgrn, state_ffn_conv, norm_mix_g, pool_w, pool_scale, hgrn_w_in, hgrn_lb_logits, hgrn_norm_g, hgrn_w_out, norm_ffn_g, ffn_w_up, ffn_conv_w, ffn_conv_b, ffn_w_down, norm_out_g):
    depth, D = norm_mix_g.shape
    ff = ffn_conv_b.shape[-1]
    nh = D // HEAD_DIM
    Bp = x_prompt.shape[0]
    gw = D // len(POOL_WINDOWS)

    lane_win = jnp.repeat(jnp.asarray(POOL_WINDOWS, F32), gw)[None, :]
    pos = jnp.arange(POOL_HIST, dtype=F32)[:, None] + 1.0
    w = dict(
        norm_mix_g=norm_mix_g[:, None, :], norm_ffn_g=norm_ffn_g[:, None, :],
        norm_out_g=norm_out_g[None, None, :],
        pool_w=pool_w.astype(BF16), pool_scale=pool_scale[:, None, :],
        pool_invw=(1.0 / lane_win)[None], pool_inv16=(1.0 / jnp.minimum(lane_win, pos))[None],
        hgrn_w_in=hgrn_w_in.astype(BF16), hgrn_lb_logits=hgrn_lb_logits,
        hgrn_norm_g=hgrn_norm_g[:, None, :], hgrn_w_out=hgrn_w_out.astype(BF16),
        ffn_w_up=ffn_w_up.astype(BF16), ffn_conv_w=ffn_conv_w,
        ffn_conv_b=ffn_conv_b[:, None, :], ffn_w_down=ffn_w_down.astype(BF16))

    y_p, pool_p, hgrn_p, conv_p = _trunk(
        x_prompt,
        jnp.zeros((1, Bp, POOL_HIST, D), F32),
        jnp.zeros((1, Bp, nh, HEAD_DIM, HEAD_DIM), F32),
        jnp.zeros((1, Bp, CONV_HIST, ff), F32),
        True, w)

    pool_pad = POOL_HIST - state_pool.shape[2]
    conv_pad = CONV_HIST - state_ffn_conv.shape[2]
    y_s, pool_s, hgrn_s, conv_s = _trunk(
        x_sample,
        jnp.pad(state_pool, ((0, 0), (0, 0), (pool_pad, 0), (0, 0))),
        state_hgrn,
        jnp.pad(state_ffn_conv, ((0, 0), (0, 0), (conv_pad, 0), (0, 0))),
        False, w)

    return (y_p, y_s, pool_p, pool_s, hgrn_p, hgrn_s, conv_p, conv_s)
```

```python
import functools

import jax
import jax.numpy as jnp
from jax import lax
from jax.experimental import pallas as pl
from jax.experimental.pallas import tpu as pltpu

F32 = jnp.float32
BF16 = jnp.bfloat16

EPS = 1e-6
POOL_WINDOWS = (2, 4, 8, 16)
POOL_HIST = 16
CONV_HIST = 8
CONV_W = 3
HEAD_DIM = 128
CHUNK = 64
EXP2_CLAMP = 120.0
F32_TINY = 2.0 ** -126
MXU_WIDTH = 256
FFN_CHUNK = MXU_WIDTH
FFN_TILE_ROWS = 512
POOL_BLOCK_ROWS = 512
HGRN_TILE_ROWS = 512
HGRN_PART_ROWS = 256
VMEM_LIMIT_BYTES = 56 * 1024 * 1024


def _rmsnorm(x, g):
    ms = jnp.mean(x * x, axis=-1, keepdims=True)
    return x * lax.rsqrt(ms + EPS) * g


def _sigmoid(x):
    return 0.5 + 0.5 * jnp.tanh(0.5 * x)


def _silu(x):
    h = 0.5 * x
    return h + h * jnp.tanh(h)


def _dot_cols(lhs, w_ref, c0, c1):
    return jnp.concatenate(
        [jnp.dot(lhs, w_ref[:, c:c + MXU_WIDTH], preferred_element_type=F32)
         for c in range(c0, c1, MXU_WIDTH)], axis=1)


def _tile_plan(batch, seq_len, rows):
    tl = min(seq_len, rows)
    assert seq_len % tl == 0 and tl % CHUNK == 0
    nseq = max(1, min(batch, rows // tl))
    while batch % nseq:
        nseq -= 1
    return nseq, tl


def _layer_spec(arr, layer, *tail_block_idx):
    tail = arr.shape[1:]
    idx = tuple(tail_block_idx) + (0,) * (len(tail) - len(tail_block_idx))
    return pl.BlockSpec((None,) + tuple(tail), lambda b, j: (layer,) + idx,
                        pipeline_mode=pl.Buffered(1))


def _state_spec(arr, layer, nseq):
    tail = arr.shape[2:]
    return pl.BlockSpec((None, nseq) + tuple(tail),
                        lambda b, j: (layer, b) + (0,) * len(tail))


def _rows_spec(nseq, tl, d):
    return pl.BlockSpec((nseq, tl, d), lambda b, j: (b, j, 0))


def _params():
    return pltpu.CompilerParams(
        dimension_semantics=("arbitrary", "arbitrary"),
        vmem_limit_bytes=VMEM_LIMIT_BYTES)


def _pool_steps(x_ref, dst_ref, g_ref, w_ref, scale_ref, invw_ref, inv16_ref, last_ref,
                carry_ref, hn_ref, *, ramp, commit):
    nseq, tl, d_model = x_ref.shape
    gw = w_ref.shape[-1]
    assert all(b == 2 * a for a, b in zip((1,) + POOL_WINDOWS, POOL_WINDOWS))
    assert POOL_WINDOWS[-1] <= POOL_HIST

    seg = min(tl, POOL_BLOCK_ROWS)
    segments = [(s, r0, seg) for s in range(nseq) for r0 in range(0, tl, seg)]
    per_block = max(1, POOL_BLOCK_ROWS // seg)
    blocks = [segments[i:i + per_block] for i in range(0, len(segments), per_block)]

    def norm_step(block):
        for s, r0, n in block:
            rows = slice(s * tl + r0, s * tl + r0 + n)
            hn_ref[rows, :] = _rmsnorm(x_ref[s, r0:r0 + n, :], g_ref[...])

    def group_step(gi, block):
        sl = slice(gi * gw, (gi + 1) * gw)
        for s, r0, n in block:
            first = s * tl + r0
            rows = slice(first, first + n)
            hn = hn_ref[rows, sl]
            prev = carry_ref[s, :, sl] if r0 == 0 else hn_ref[first - POOL_HIST:first, sl]
            level = jnp.concatenate([prev, hn], axis=0)
            for k in range(gi + 1):
                level = level + pltpu.roll(level, 2 ** k, axis=0)
            ws = level[POOL_HIST:]
            d = ws * invw_ref[:, sl] - hn
            if ramp and r0 == 0:
                head = ws[0:POOL_HIST] * inv16_ref[:, sl] - hn[0:POOL_HIST]
                d = jnp.concatenate([head, d[POOL_HIST:]], axis=0)
            y = jnp.dot(d.astype(BF16), w_ref[gi], preferred_element_type=F32)
            dst_ref[rows, sl] = x_ref[s, r0:r0 + n, sl] + y * scale_ref[:, sl]

    def advance():
        for s in range(nseq):
            tail = hn_ref[(s + 1) * tl - POOL_HIST:(s + 1) * tl, :]
            last_ref[s] = tail
            carry_ref[s] = tail

    def commit_step():
        if commit is None:
            advance()
        else:
            pl.when(commit)(advance)

    return ([functools.partial(norm_step, blk) for blk in blocks]
            + [functools.partial(group_step, gi, blk) for blk in blocks
               for gi in range(len(POOL_WINDOWS))]
            + [commit_step])


def _conv_ffn(j, x, hist_ref, g_ref, wg_ref, wv_ref, cw_ref, cb_ref, wd_ref,
              last_ref, gate_ref, hid_ref, side_steps=()):
    side_steps = list(side_steps)
    nseq, tl = gate_ref.shape[0], gate_ref.shape[1] - CONV_HIST
    m, d_model = x.shape
    ff = wg_ref.shape[-1]

    @pl.when(j == 0)
    def _():
        gate_ref[:, 0:CONV_HIST, :] = hist_ref[...]

    hn = _rmsnorm(x, g_ref[...]).astype(BF16)

    for c0 in range(0, ff, FFN_CHUNK):
        sl = slice(c0, c0 + FFN_CHUNK)
        gate = jnp.dot(hn, wg_ref[:, sl], preferred_element_type=F32)
        val = jnp.dot(hn, wv_ref[:, sl], preferred_element_type=F32)
        gate_ref[:, CONV_HIST:CONV_HIST + tl, sl] = gate.reshape(nseq, tl, FFN_CHUNK)
        conv = cb_ref[:, sl] + cw_ref[CONV_W - 1:CONV_W, sl] * gate
        for k in range(1, CONV_W):
            prev = gate_ref[:, CONV_HIST - k:CONV_HIST - k + tl, sl].reshape(m, FFN_CHUNK)
            conv = conv + cw_ref[CONV_W - 1 - k:CONV_W - k, sl] * prev
        hid_ref[:, sl] = (_silu(conv) * val).astype(BF16)

    tail = gate_ref[:, tl:tl + CONV_HIST, :]
    last_ref[...] = tail
    gate_ref[:, 0:CONV_HIST, :] = tail
    hid = hid_ref[...]
    n_blocks = d_model // MXU_WIDTH
    cols = []
    for c in range(n_blocks):
        cols.append(jnp.dot(hid, wd_ref[:, c * MXU_WIDTH:(c + 1) * MXU_WIDTH],
                            preferred_element_type=F32))
        for _ in range(-(-len(side_steps) // (n_blocks - c))):
            side_steps.pop(0)()
    return x + jnp.concatenate(cols, axis=1)


def _rows(x_ref):
    x = x_ref[...]
    return x.reshape(-1, x.shape[-1])


def _store_rows(out_ref, y, final_norm, gout_ref):
    if final_norm:
        y = _rmsnorm(y, gout_ref[...])
    out_ref[...] = y.reshape(out_ref.shape)


def _ffn_kernel(final_norm, x_ref, chist_ref, gffn_ref, wg_ref, wv_ref, cw_ref,
                cb_ref, wd_ref, gout_ref, out_ref, clast_ref, gate_ref, hid_ref):
    j = pl.program_id(1)
    y = _conv_ffn(j, _rows(x_ref), chist_ref, gffn_ref, wg_ref, wv_ref, cw_ref,
                  cb_ref, wd_ref, clast_ref, gate_ref, hid_ref)
    _store_rows(out_ref, y, final_norm, gout_ref)


def _pool_ffn_kernel(ramp, final_norm, lookahead, xnext_ref, xfirst_ref, phist_ref, gmix_ref,
                     pw_ref, pscale_ref, invw_ref, inv16_ref, chist_ref, gffn_ref,
                     wg_ref, wv_ref, cw_ref, cb_ref, wd_ref, gout_ref, out_ref,
                     plast_ref, clast_ref, carry_ref, hn_ref, mixed_ref, gate_ref, hid_ref):
    j = pl.program_id(1)
    nj = pl.num_programs(1)
    pool = functools.partial(_pool_steps, g_ref=gmix_ref, w_ref=pw_ref, scale_ref=pscale_ref,
                             invw_ref=invw_ref, inv16_ref=inv16_ref, last_ref=plast_ref,
                             carry_ref=carry_ref, hn_ref=hn_ref)

    @pl.when(j == 0)
    def _():
        carry_ref[...] = phist_ref[...]
        for step in pool(xfirst_ref, mixed_ref.at[0], ramp=ramp, commit=None):
            step()

    slot = lax.rem(j, 2)
    pool_next = (pool(xnext_ref, mixed_ref.at[1 - slot], ramp=False, commit=j + 1 < nj)
                 if lookahead else ())
    y = _conv_ffn(j, mixed_ref[slot], chist_ref, gffn_ref, wg_ref, wv_ref, cw_ref, cb_ref,
                  wd_ref, clast_ref, gate_ref, hid_ref, side_steps=pool_next)
    _store_rows(out_ref, y, final_norm, gout_ref)


def _ffn_operands(w, layer, conv_hist, conv_layer, nseq):
    ff = w["ffn_conv_b"].shape[-1]
    args = [conv_hist, w["norm_ffn_g"], w["ffn_w_up"], w["ffn_w_up"], w["ffn_conv_w"],
            w["ffn_conv_b"], w["ffn_w_down"], w["norm_out_g"]]
    d_model = w["norm_out_g"].shape[-1]
    specs = [
        _state_spec(conv_hist, conv_layer, nseq),
        _layer_spec(w["norm_ffn_g"], layer),
        pl.BlockSpec((None, d_model, ff), lambda b, j: (layer, 0, 0),
                     pipeline_mode=pl.Buffered(1)),
        pl.BlockSpec((None, d_model, ff), lambda b, j: (layer, 0, 1),
                     pipeline_mode=pl.Buffered(1)),
        _layer_spec(w["ffn_conv_w"], layer),
        _layer_spec(w["ffn_conv_b"], layer),
        _layer_spec(w["ffn_w_down"], layer),
        _layer_spec(w["norm_out_g"], 0),
    ]
    return args, specs


def _ffn_scratch(nseq, tl, ff):
    return [pltpu.VMEM((nseq, tl + CONV_HIST, ff), F32),
            pltpu.VMEM((nseq * tl, ff), BF16)]


def _ffn_layer(x, conv_hist, conv_layer, w, layer, final_norm):
    B, L, D = x.shape
    ff = w["ffn_conv_b"].shape[-1]
    nseq, tl = _tile_plan(B, L, FFN_TILE_ROWS)
    args, specs = _ffn_operands(w, layer, conv_hist, conv_layer, nseq)
    return pl.pallas_call(
        functools.partial(_ffn_kernel, final_norm),
        grid=(B // nseq, L // tl),
        in_specs=[_rows_spec(nseq, tl, D)] + specs,
        out_specs=[_rows_spec(nseq, tl, D),
                   pl.BlockSpec((nseq, CONV_HIST, ff), lambda b, j: (b, 0, 0))],
        out_shape=[jax.ShapeDtypeStruct((B, L, D), F32),
                   jax.ShapeDtypeStruct((B, CONV_HIST, ff), F32)],
        scratch_shapes=_ffn_scratch(nseq, tl, ff),
        compiler_params=_params(),
        name="ffn_layer",
    )(x, *args)


def _pool_ffn_layer(x, pool_hist, pool_layer, conv_hist, conv_layer, w, layer, jm,
                    ramp, final_norm):
    B, L, D = x.shape
    ff = w["ffn_conv_b"].shape[-1]
    nseq, tl = _tile_plan(B, L, FFN_TILE_ROWS)
    ffn_args, ffn_specs = _ffn_operands(w, layer, conv_hist, conv_layer, nseq)
    args = [pool_hist, w["norm_mix_g"], w["pool_w"], w["pool_scale"], w["pool_invw"],
            w["pool_inv16"]] + ffn_args
    specs = [
        _state_spec(pool_hist, pool_layer, nseq),
        _layer_spec(w["norm_mix_g"], layer),
        _layer_spec(w["pool_w"], jm),
        _layer_spec(w["pool_scale"], jm),
        _layer_spec(w["pool_invw"], 0),
        _layer_spec(w["pool_inv16"], 0),
    ] + ffn_specs
    nt = L // tl
    return pl.pallas_call(
        functools.partial(_pool_ffn_kernel, ramp, final_norm, nt > 1),
        grid=(B // nseq, nt),
        in_specs=[
            pl.BlockSpec((nseq, tl, D), lambda b, j: (b, jnp.minimum(j + 1, nt - 1), 0)),
            pl.BlockSpec((nseq, tl, D), lambda b, j: (b, 0, 0)),
        ] + specs,
        out_specs=[_rows_spec(nseq, tl, D),
                   pl.BlockSpec((nseq, POOL_HIST, D), lambda b, j: (b, 0, 0)),
                   pl.BlockSpec((nseq, CONV_HIST, ff), lambda b, j: (b, 0, 0))],
        out_shape=[jax.ShapeDtypeStruct((B, L, D), F32),
                   jax.ShapeDtypeStruct((B, POOL_HIST, D), F32),
                   jax.ShapeDtypeStruct((B, CONV_HIST, ff), F32)],
        scratch_shapes=[pltpu.VMEM((nseq, POOL_HIST, D), F32),
                        pltpu.VMEM((nseq * tl, D), F32),
                        pltpu.VMEM((2, nseq * tl, D), F32),
                        ] + _ffn_scratch(nseq, tl, ff),
        compiler_params=_params(),
        name="pool_ffn_layer",
    )(x, x, *args)


def _cumsum_rows(g):
    rows, width = g.shape
    sub = lax.broadcasted_iota(jnp.int32, (8, width), 0)
    outs = []
    carry = None
    for r0 in range(0, rows, 8):
        x = g[r0:r0 + 8]
        for sh in (1, 2, 4):
            x = x + jnp.where(sub >= sh, pltpu.roll(x, sh, axis=0), 0.0)
        if carry is not None:
            x = x + carry
        carry = x[7:8]
        outs.append(x)
    return jnp.concatenate(outs, axis=0)


def _block_rel(b, block, ref_fn):
    pieces = []
    for r0 in range(0, b.shape[0], block):
        pieces.append(b[r0:r0 + block] - ref_fn(r0))
    return jnp.concatenate(pieces, axis=0)


def _hgrn_kernel(layer_idx, x_ref, s0_ref, lbl_ref, g_ref, win_ref, gn_ref,
                 wo_ref, out_ref, sfin_ref, st_ref, o_ref):
    j = pl.program_id(1)
    nj = pl.num_programs(1)
    nseq, tl, D = x_ref.shape
    nh = D // HEAD_DIM

    @pl.when(j == 0)
    def _():
        for s in range(nseq):
            for h in range(nh):
                st_ref[s, h] = s0_ref[s, h].T

    logits = lbl_ref[...]
    e = jnp.exp(logits - jnp.max(logits, axis=0, keepdims=True))
    lb = jnp.sum(e[1:layer_idx + 1], axis=0, keepdims=True) / jnp.sum(e, axis=0, keepdims=True)

    x = _rows(x_ref)
    part_rows = min(x.shape[0], HGRN_PART_ROWS)
    nparts = x.shape[0] // part_rows

    npairs = D // MXU_WIDTH
    heads_per_pair = MXU_WIDTH // HEAD_DIM

    def proj(hn, which, c):
        c0 = which * D + c * MXU_WIDTH
        return jnp.dot(hn, win_ref[:, c0:c0 + MXU_WIDTH], preferred_element_type=F32)

    def new_part(p):
        xp = x[p * part_rows:(p + 1) * part_rows]
        return dict(x=xp, hn=_rmsnorm(xp, g_ref[...]).astype(BF16), pairs={})

    def project_steps(part):
        def q_step(c):
            part["pairs"][c] = dict(q=_silu(proj(part["hn"], 0, c)))

        def f_step(c):
            one_minus_lb = 1.0 - lb[:, c * MXU_WIDTH:(c + 1) * MXU_WIDTH]
            kk = one_minus_lb * _sigmoid(-proj(part["hn"], 1, c))
            forget = jnp.maximum(1.0 - kk, F32_TINY)
            part["pairs"][c].update(kk=kk, glog=jnp.log2(forget))

        def v_step(c):
            part["pairs"][c]["v"] = proj(part["hn"], 2, c).astype(BF16)

        return [functools.partial(step, c) for c in range(npairs)
                for step in (q_step, f_step, v_step)]

    row = lax.broadcasted_iota(jnp.int32, (CHUNK, CHUNK), 0)
    col = lax.broadcasted_iota(jnp.int32, (CHUNK, CHUNK), 1)
    half, blk = CHUNK // 2, CHUNK // 4
    row_blk = jnp.right_shift(row, blk.bit_length() - 1)
    col_blk = jnp.right_shift(col, blk.bit_length() - 1)
    mask2 = ((row >= half) == (col >= half)) & (row_blk > col_blk)
    mask3 = (row_blk == col_blk) & (row >= col)
    nt_dims = (((1,), (1,)), ((), ()))
    tn_dims = (((0,), (0,)), ((), ()))
    zeros_half = jnp.zeros((half, HEAD_DIM), BF16)
    zeros_blk = jnp.zeros((blk, HEAD_DIM), BF16)

    def scores(part, r0, h):
        rs = slice(r0 % part_rows, r0 % part_rows + CHUNK)
        pair = part["pairs"][h // heads_per_pair]
        hs = slice(h % heads_per_pair * HEAD_DIM, (h % heads_per_pair + 1) * HEAD_DIM)
        b = _cumsum_rows(pair["glog"][rs, hs])
        b_end = b[CHUNK - 1:CHUNK]
        qc = pair["q"][rs, hs]
        kc = pair["kk"][rs, hs]
        vc = pair["v"][rs, hs]
        q1 = (qc[half:] * jnp.exp2(b[half:] - b[half:half + 1])).astype(BF16)
        k1 = (kc[:half] * jnp.exp2(b[half:half + 1] - b[:half])).astype(BF16)
        m1 = lax.dot_general(q1, jnp.concatenate([k1, zeros_half], axis=0), nt_dims,
                             preferred_element_type=F32)
        q2, k2 = [], []
        for h0 in (0, half):
            ref = b[h0 + blk:h0 + blk + 1]
            lo, hi = slice(h0, h0 + blk), slice(h0 + blk, h0 + half)
            q2 += [zeros_blk, (qc[hi] * jnp.exp2(b[hi] - ref)).astype(BF16)]
            k2 += [(kc[lo] * jnp.exp2(ref - b[lo])).astype(BF16), zeros_blk]
        m2 = lax.dot_general(jnp.concatenate(q2, axis=0), jnp.concatenate(k2, axis=0),
                             nt_dims, preferred_element_type=F32)
        e3 = _block_rel(b, blk, lambda i: 0.5 * (b[i:i + 1] + b[i + blk - 1:i + blk]))
        e3 = jnp.clip(e3, -EXP2_CLAMP, EXP2_CLAMP)
        q3 = (qc * jnp.exp2(e3)).astype(BF16)
        k3 = (kc * jnp.exp2(-e3)).astype(BF16)
        m3 = lax.dot_general(q3, k3, nt_dims, preferred_element_type=F32)
        a = jnp.where(mask2, m2, jnp.where(mask3, m3, 0.0))
        a = jnp.concatenate([a[:half], a[half:] + m1], axis=0)
        q_in = (qc * jnp.exp2(b)).astype(BF16)
        k_up = (kc * jnp.exp2(b_end - b)).astype(BF16)
        upd = lax.dot_general(vc, k_up, tn_dims, preferred_element_type=F32)
        return a.astype(BF16), q_in, vc, upd, jnp.exp2(b_end)

    def outputs(s, r0, h, a, q_in, vc, upd, dec):
        hs = slice(h * HEAD_DIM, (h + 1) * HEAD_DIM)
        st = st_ref[s, h]
        o_in = lax.dot_general(q_in, st.astype(BF16), nt_dims, preferred_element_type=F32)
        o_ref[r0:r0 + CHUNK, hs] = o_in + jnp.dot(a, vc, preferred_element_type=F32)
        st_ref[s, h] = st * dec + upd

    def gate_pair(p, part, c):
        rows = slice(p * part_rows, (p + 1) * part_rows)
        gate = _silu(proj(part["hn"], 3, c))
        for hp in range(heads_per_pair):
            hs = slice(c * MXU_WIDTH + hp * HEAD_DIM, c * MXU_WIDTH + (hp + 1) * HEAD_DIM)
            oh = o_ref[rows, hs]
            ms = jnp.mean(oh * oh, axis=-1, keepdims=True)
            o_ref[rows, hs] = (oh * lax.rsqrt(ms + EPS) * gn_ref[:, hs]
                               * gate[:, hp * HEAD_DIM:(hp + 1) * HEAD_DIM])

    def out_cols(p, part, c):
        rows = slice(p * part_rows, (p + 1) * part_rows)
        cs = slice(c * MXU_WIDTH, (c + 1) * MXU_WIDTH)
        y = part["x"][:, cs] + jnp.dot(o_ref[rows, :].astype(BF16), wo_ref[:, cs],
                                       preferred_element_type=F32)
        if nseq == 1:
            out_ref[0, rows, cs] = y
        else:
            seqs = slice(p * part_rows // tl, (p + 1) * part_rows // tl)
            out_ref[seqs, :, cs] = y.reshape(part_rows // tl, tl, MXU_WIDTH)

    def finish_steps(p, part):
        steps = [functools.partial(gate_pair, p, part, c) for c in range(npairs)]
        return steps + [functools.partial(out_cols, p, part, c) for c in range(npairs)]

    chunks = [(s, s * tl + c * CHUNK) for s in range(nseq) for c in range(tl // CHUNK)]
    parts = {0: new_part(0)}
    for step in project_steps(parts[0]):
        step()
    pending = [scores(parts[0], chunks[0][1], h) for h in range(nh)]
    for i, (s, r0) in enumerate(chunks):
        p = r0 // part_rows
        if r0 % part_rows == 0 and p + 1 < nparts:
            parts[p + 1] = new_part(p + 1)
            for step in project_steps(parts[p + 1]):
                step()
        ready = pending
        pending = []
        for h in range(nh):
            if i + 1 < len(chunks):
                nxt = chunks[i + 1][1]
                pending.append(scores(parts[nxt // part_rows], nxt, h))
            outputs(s, r0, h, *ready[h])
        if (r0 + CHUNK) % part_rows == 0:
            for step in finish_steps(p, parts.pop(p)):
                step()

    @pl.when(j == nj - 1)
    def _():
        for s in range(nseq):
            for h in range(nh):
                sfin_ref[s, h] = st_ref[s, h].T


def _hgrn_layer(x, state, state_layer, w, layer, jm):
    B, L, D = x.shape
    nh = D // HEAD_DIM
    nseq, tl = _tile_plan(B, L, HGRN_TILE_ROWS)
    out_state_spec = pl.BlockSpec((nseq, nh, HEAD_DIM, HEAD_DIM), lambda b, j: (b, 0, 0, 0))
    return pl.pallas_call(
        functools.partial(_hgrn_kernel, layer),
        grid=(B // nseq, L // tl),
        in_specs=[
            _rows_spec(nseq, tl, D),
            _state_spec(state, state_layer, nseq),
            pl.BlockSpec(w["hgrn_lb_logits"].shape, lambda b, j: (0, 0),
                         pipeline_mode=pl.Buffered(1)),
            _layer_spec(w["norm_mix_g"], layer),
            _layer_spec(w["hgrn_w_in"], jm),
            _layer_spec(w["hgrn_norm_g"], jm),
            _layer_spec(w["hgrn_w_out"], jm),
        ],
        out_specs=[_rows_spec(nseq, tl, D), out_state_spec],
        out_shape=[jax.ShapeDtypeStruct((B, L, D), F32),
                   jax.ShapeDtypeStruct((B, nh, HEAD_DIM, HEAD_DIM), F32)],
        scratch_shapes=[pltpu.VMEM((nseq, nh, HEAD_DIM, HEAD_DIM), F32),
                        pltpu.VMEM((nseq * tl, D), F32)],
        compiler_params=_params(),
        name="hgrn_layer",
    )(x, state, w["hgrn_lb_logits"], w["norm_mix_g"], w["hgrn_w_in"], w["hgrn_norm_g"],
      w["hgrn_w_out"])


def _trunk(x, pool_hist, hgrn_state, conv_hist, fresh, w):
    depth = w["norm_mix_g"].shape[0]
    new_pool, new_hgrn, new_conv = [], [], []
    for i in range(depth):
        jm = i // 2
        final = i == depth - 1
        conv_layer = 0 if fresh else i
        if i % 2 == 0:
            x, p_last, c_last = _pool_ffn_layer(
                x, pool_hist, 0 if fresh else jm, conv_hist, conv_layer, w, i, jm,
                ramp=fresh, final_norm=final)
            new_pool.append(p_last[:, 1:, :])
        else:
            x, s_fin = _hgrn_layer(x, hgrn_state, 0 if fresh else jm, w, i, jm)
            new_hgrn.append(s_fin)
            x, c_last = _ffn_layer(x, conv_hist, conv_layer, w, i, final)
        new_conv.append(c_last[:, CONV_HIST - (CONV_W - 1):, :])
    return x, jnp.stack(new_pool), jnp.stack(new_hgrn), jnp.stack(new_conv)


def kernel(x_prompt, x_sample, state_pool, state_hgrn, state_ffn_conv, norm_mix_g, pool_w, pool_scale, hgrn_w_in, hgrn_lb_logits, hgrn_norm_g, hgrn_w_out, norm_ffn_g, ffn_w_up, ffn_conv_w, ffn_conv_b, ffn_w_down, norm_out_g):
    depth, D = norm_mix_g.shape
    ff = ffn_conv_b.shape[-1]
    nh = D // HEAD_DIM
    Bp = x_prompt.shape[0]
    gw = D // len(POOL_WINDOWS)

    lane_win = jnp.repeat(jnp.asarray(POOL_WINDOWS, F32), gw)[None, :]
    pos = jnp.arange(POOL_HIST, dtype=F32)[:, None] + 1.0
    w = dict(
        norm_mix_g=norm_mix_g[:, None, :], norm_ffn_g=norm_ffn_g[:, None, :],
        norm_out_g=norm_out_g[None, None, :],
        pool_w=pool_w.astype(BF16), pool_scale=pool_scale[:, None, :],
        pool_invw=(1.0 / lane_win)[None], pool_inv16=(1.0 / jnp.minimum(lane_win, pos))[None],
        hgrn_w_in=hgrn_w_in.astype(BF16), hgrn_lb_logits=hgrn_lb_logits,
        hgrn_norm_g=hgrn_norm_g[:, None, :], hgrn_w_out=hgrn_w_out.astype(BF16),
        ffn_w_up=ffn_w_up.astype(BF16), ffn_conv_w=ffn_conv_w,
        ffn_conv_b=ffn_conv_b[:, None, :], ffn_w_down=ffn_w_down.astype(BF16))

    y_p, pool_p, hgrn_p, conv_p = _trunk(
        x_prompt,
        jnp.zeros((1, Bp, POOL_HIST, D), F32),
        jnp.zeros((1, Bp, nh, HEAD_DIM, HEAD_DIM), F32),
        jnp.zeros((1, Bp, CONV_HIST, ff), F32),
        True, w)

    pool_pad = POOL_HIST - state_pool.shape[2]
    conv_pad = CONV_HIST - state_ffn_conv.shape[2]
    y_s, pool_s, hgrn_s, conv_s = _trunk(
        x_sample,
        jnp.pad(state_pool, ((0, 0), (0, 0), (pool_pad, 0), (0, 0))),
        state_hgrn,
        jnp.pad(state_ffn_conv, ((0, 0), (0, 0), (conv_pad, 0), (0, 0))),
        False, w)

    return (y_p, y_s, pool_p, pool_s, hgrn_p, hgrn_s, conv_p, conv_s)
```

```python
import functools
import math

import jax
import jax.numpy as jnp
from jax import lax
from jax.experimental import pallas as pl
from jax.experimental.pallas import tpu as pltpu

F32 = jnp.float32
BF16 = jnp.bfloat16

EPS = 1e-6
POOL_WINDOWS = (2, 4, 8, 16)
POOL_HIST = 16
CONV_HIST = 8
CONV_W = 3
HEAD_DIM = 128
CHUNK = 64
FORGET_FLOOR = 2.0 ** -15
assert -math.log2(FORGET_FLOOR) * (CHUNK // 4) / 2 <= 126
MXU_WIDTH = 256
FFN_CHUNK = MXU_WIDTH
FFN_TILE_ROWS = 512
POOL_BLOCK_ROWS = 512
HGRN_TILE_ROWS = 512
HGRN_PART_ROWS = 256
VMEM_LIMIT_BYTES = 56 * 1024 * 1024


def _rmsnorm(x, g):
    ms = jnp.mean(x * x, axis=-1, keepdims=True)
    return x * lax.rsqrt(ms + EPS) * g


def _sigmoid(x):
    return 0.5 + 0.5 * jnp.tanh(0.5 * x)


def _silu(x):
    h = 0.5 * x
    return h + h * jnp.tanh(h)


def _dot_cols(lhs, w_ref, c0, c1):
    return jnp.concatenate(
        [jnp.dot(lhs, w_ref[:, c:c + MXU_WIDTH], preferred_element_type=F32)
         for c in range(c0, c1, MXU_WIDTH)], axis=1)


def _tile_plan(batch, seq_len, rows):
    tl = min(seq_len, rows)
    assert seq_len % tl == 0 and tl % CHUNK == 0
    nseq = max(1, min(batch, rows // tl))
    while batch % nseq:
        nseq -= 1
    return nseq, tl


def _layer_spec(arr, layer, *tail_block_idx):
    tail = arr.shape[1:]
    idx = tuple(tail_block_idx) + (0,) * (len(tail) - len(tail_block_idx))
    return pl.BlockSpec((None,) + tuple(tail), lambda b, j: (layer,) + idx,
                        pipeline_mode=pl.Buffered(1))


def _state_spec(arr, layer, nseq):
    tail = arr.shape[2:]
    return pl.BlockSpec((None, nseq) + tuple(tail),
                        lambda b, j: (layer, b) + (0,) * len(tail))


def _rows_spec(nseq, tl, d):
    return pl.BlockSpec((nseq, tl, d), lambda b, j: (b, j, 0))


def _params():
    return pltpu.CompilerParams(
        dimension_semantics=("arbitrary", "arbitrary"),
        vmem_limit_bytes=VMEM_LIMIT_BYTES)


def _pool_steps(x_ref, dst_ref, g_ref, w_ref, scale_ref, invw_ref, inv16_ref, last_ref,
                carry_ref, hn_ref, *, ramp, commit):
    nseq, tl, d_model = x_ref.shape
    gw = w_ref.shape[-1]
    assert all(b == 2 * a for a, b in zip((1,) + POOL_WINDOWS, POOL_WINDOWS))
    assert POOL_WINDOWS[-1] <= POOL_HIST

    seg = min(tl, POOL_BLOCK_ROWS)
    segments = [(s, r0, seg) for s in range(nseq) for r0 in range(0, tl, seg)]
    per_block = max(1, POOL_BLOCK_ROWS // seg)
    blocks = [segments[i:i + per_block] for i in range(0, len(segments), per_block)]

    def norm_step(block):
        for s, r0, n in block:
            rows = slice(s * tl + r0, s * tl + r0 + n)
            hn_ref[rows, :] = _rmsnorm(x_ref[s, r0:r0 + n, :], g_ref[...])

    def group_step(gi, block):
        sl = slice(gi * gw, (gi + 1) * gw)
        for s, r0, n in block:
            first = s * tl + r0
            rows = slice(first, first + n)
            hn = hn_ref[rows, sl]
            prev = carry_ref[s, :, sl] if r0 == 0 else hn_ref[first - POOL_HIST:first, sl]
            level = jnp.concatenate([prev, hn], axis=0)
            for k in range(gi + 1):
                level = level + pltpu.roll(level, 2 ** k, axis=0)
            ws = level[POOL_HIST:]
            d = ws * invw_ref[:, sl] - hn
            if ramp and r0 == 0:
                head = ws[0:POOL_HIST] * inv16_ref[:, sl] - hn[0:POOL_HIST]
                d = jnp.concatenate([head, d[POOL_HIST:]], axis=0)
            y = jnp.dot(d.astype(BF16), w_ref[gi], preferred_element_type=F32)
            dst_ref[rows, sl] = x_ref[s, r0:r0 + n, sl] + y * scale_ref[:, sl]

    def advance():
        for s in range(nseq):
            tail = hn_ref[(s + 1) * tl - POOL_HIST:(s + 1) * tl, :]
            last_ref[s] = tail
            carry_ref[s] = tail

    def commit_step():
        if commit is None:
            advance()
        else:
            pl.when(commit)(advance)

    return ([functools.partial(norm_step, blk) for blk in blocks]
            + [functools.partial(group_step, gi, blk) for blk in blocks
               for gi in range(len(POOL_WINDOWS))]
            + [commit_step])


def _conv_ffn(j, x, hist_ref, g_ref, wg_ref, wv_ref, cw_ref, cb_ref, wd_ref,
              last_ref, gate_ref, hid_ref, side_steps=()):
    side_steps = list(side_steps)
    nseq, tl = gate_ref.shape[0], gate_ref.shape[1] - CONV_HIST
    m, d_model = x.shape
    ff = wg_ref.shape[-1]

    @pl.when(j == 0)
    def _():
        gate_ref[:, 0:CONV_HIST, :] = hist_ref[...]

    hn = _rmsnorm(x, g_ref[...]).astype(BF16)

    for c0 in range(0, ff, FFN_CHUNK):
        sl = slice(c0, c0 + FFN_CHUNK)
        gate = jnp.dot(hn, wg_ref[:, sl], preferred_element_type=F32)
        val = jnp.dot(hn, wv_ref[:, sl], preferred_element_type=F32)
        gate_ref[:, CONV_HIST:CONV_HIST + tl, sl] = gate.reshape(nseq, tl, FFN_CHUNK)
        conv = cb_ref[:, sl] + cw_ref[CONV_W - 1:CONV_W, sl] * gate
        for k in range(1, CONV_W):
            prev = gate_ref[:, CONV_HIST - k:CONV_HIST - k + tl, sl].reshape(m, FFN_CHUNK)
            conv = conv + cw_ref[CONV_W - 1 - k:CONV_W - k, sl] * prev
        hid_ref[:, sl] = (_silu(conv) * val).astype(BF16)

    tail = gate_ref[:, tl:tl + CONV_HIST, :]
    last_ref[...] = tail
    gate_ref[:, 0:CONV_HIST, :] = tail
    hid = hid_ref[...]
    n_blocks = d_model // MXU_WIDTH
    cols = []
    for c in range(n_blocks):
        cols.append(jnp.dot(hid, wd_ref[:, c * MXU_WIDTH:(c + 1) * MXU_WIDTH],
                            preferred_element_type=F32))
        for _ in range(-(-len(side_steps) // (n_blocks - c))):
            side_steps.pop(0)()
    return x + jnp.concatenate(cols, axis=1)


def _rows(x_ref):
    x = x_ref[...]
    return x.reshape(-1, x.shape[-1])


def _store_rows(out_ref, y, final_norm, gout_ref):
    if final_norm:
        y = _rmsnorm(y, gout_ref[...])
    out_ref[...] = y.reshape(out_ref.shape)


def _ffn_kernel(final_norm, x_ref, chist_ref, gffn_ref, wg_ref, wv_ref, cw_ref,
                cb_ref, wd_ref, gout_ref, out_ref, clast_ref, gate_ref, hid_ref):
    j = pl.program_id(1)
    y = _conv_ffn(j, _rows(x_ref), chist_ref, gffn_ref, wg_ref, wv_ref, cw_ref,
                  cb_ref, wd_ref, clast_ref, gate_ref, hid_ref)
    _store_rows(out_ref, y, final_norm, gout_ref)


def _pool_ffn_kernel(ramp, final_norm, lookahead, xnext_ref, xfirst_ref, phist_ref, gmix_ref,
                     pw_ref, pscale_ref, invw_ref, inv16_ref, chist_ref, gffn_ref,
                     wg_ref, wv_ref, cw_ref, cb_ref, wd_ref, gout_ref, out_ref,
                     plast_ref, clast_ref, carry_ref, hn_ref, mixed_ref, gate_ref, hid_ref):
    j = pl.program_id(1)
    nj = pl.num_programs(1)
    pool = functools.partial(_pool_steps, g_ref=gmix_ref, w_ref=pw_ref, scale_ref=pscale_ref,
                             invw_ref=invw_ref, inv16_ref=inv16_ref, last_ref=plast_ref,
                             carry_ref=carry_ref, hn_ref=hn_ref)

    @pl.when(j == 0)
    def _():
        carry_ref[...] = phist_ref[...]
        for step in pool(xfirst_ref, mixed_ref.at[0], ramp=ramp, commit=None):
            step()

    slot = lax.rem(j, 2)
    pool_next = (pool(xnext_ref, mixed_ref.at[1 - slot], ramp=False, commit=j + 1 < nj)
                 if lookahead else ())
    y = _conv_ffn(j, mixed_ref[slot], chist_ref, gffn_ref, wg_ref, wv_ref, cw_ref, cb_ref,
                  wd_ref, clast_ref, gate_ref, hid_ref, side_steps=pool_next)
    _store_rows(out_ref, y, final_norm, gout_ref)


def _ffn_operands(w, layer, conv_hist, conv_layer, nseq):
    ff = w["ffn_conv_b"].shape[-1]
    args = [conv_hist, w["norm_ffn_g"], w["ffn_w_up"], w["ffn_w_up"], w["ffn_conv_w"],
            w["ffn_conv_b"], w["ffn_w_down"], w["norm_out_g"]]
    d_model = w["norm_out_g"].shape[-1]
    specs = [
        _state_spec(conv_hist, conv_layer, nseq),
        _layer_spec(w["norm_ffn_g"], layer),
        pl.BlockSpec((None, d_model, ff), lambda b, j: (layer, 0, 0),
                     pipeline_mode=pl.Buffered(1)),
        pl.BlockSpec((None, d_model, ff), lambda b, j: (layer, 0, 1),
                     pipeline_mode=pl.Buffered(1)),
        _layer_spec(w["ffn_conv_w"], layer),
        _layer_spec(w["ffn_conv_b"], layer),
        _layer_spec(w["ffn_w_down"], layer),
        _layer_spec(w["norm_out_g"], 0),
    ]
    return args, specs


def _ffn_scratch(nseq, tl, ff):
    return [pltpu.VMEM((nseq, tl + CONV_HIST, ff), F32),
            pltpu.VMEM((nseq * tl, ff), BF16)]


def _ffn_layer(x, conv_hist, conv_layer, w, layer, final_norm):
    B, L, D = x.shape
    ff = w["ffn_conv_b"].shape[-1]
    nseq, tl = _tile_plan(B, L, FFN_TILE_ROWS)
    args, specs = _ffn_operands(w, layer, conv_hist, conv_layer, nseq)
    return pl.pallas_call(
        functools.partial(_ffn_kernel, final_norm),
        grid=(B // nseq, L // tl),
        in_specs=[_rows_spec(nseq, tl, D)] + specs,
        out_specs=[_rows_spec(nseq, tl, D),
                   pl.BlockSpec((nseq, CONV_HIST, ff), lambda b, j: (b, 0, 0))],
        out_shape=[jax.ShapeDtypeStruct((B, L, D), F32),
                   jax.ShapeDtypeStruct((B, CONV_HIST, ff), F32)],
        scratch_shapes=_ffn_scratch(nseq, tl, ff),
        compiler_params=_params(),
        name="ffn_layer",
    )(x, *args)


def _pool_ffn_layer(x, pool_hist, pool_layer, conv_hist, conv_layer, w, layer, jm,
                    ramp, final_norm):
    B, L, D = x.shape
    ff = w["ffn_conv_b"].shape[-1]
    nseq, tl = _tile_plan(B, L, FFN_TILE_ROWS)
    ffn_args, ffn_specs = _ffn_operands(w, layer, conv_hist, conv_layer, nseq)
    args = [pool_hist, w["norm_mix_g"], w["pool_w"], w["pool_scale"], w["pool_invw"],
            w["pool_inv16"]] + ffn_args
    specs = [
        _state_spec(pool_hist, pool_layer, nseq),
        _layer_spec(w["norm_mix_g"], layer),
        _layer_spec(w["pool_w"], jm),
        _layer_spec(w["pool_scale"], jm),
        _layer_spec(w["pool_invw"], 0),
        _layer_spec(w["pool_inv16"], 0),
    ] + ffn_specs
    nt = L // tl
    return pl.pallas_call(
        functools.partial(_pool_ffn_kernel, ramp, final_norm, nt > 1),
        grid=(B // nseq, nt),
        in_specs=[
            pl.BlockSpec((nseq, tl, D), lambda b, j: (b, jnp.minimum(j + 1, nt - 1), 0)),
            pl.BlockSpec((nseq, tl, D), lambda b, j: (b, 0, 0)),
        ] + specs,
        out_specs=[_rows_spec(nseq, tl, D),
                   pl.BlockSpec((nseq, POOL_HIST, D), lambda b, j: (b, 0, 0)),
                   pl.BlockSpec((nseq, CONV_HIST, ff), lambda b, j: (b, 0, 0))],
        out_shape=[jax.ShapeDtypeStruct((B, L, D), F32),
                   jax.ShapeDtypeStruct((B, POOL_HIST, D), F32),
                   jax.ShapeDtypeStruct((B, CONV_HIST, ff), F32)],
        scratch_shapes=[pltpu.VMEM((nseq, POOL_HIST, D), F32),
                        pltpu.VMEM((nseq * tl, D), F32),
                        pltpu.VMEM((2, nseq * tl, D), F32),
                        ] + _ffn_scratch(nseq, tl, ff),
        compiler_params=_params(),
        name="pool_ffn_layer",
    )(x, x, *args)


def _cumsum_rows(g):
    rows, width = g.shape
    sub = lax.broadcasted_iota(jnp.int32, (8, width), 0)
    outs = []
    carry = None
    for r0 in range(0, rows, 8):
        x = g[r0:r0 + 8]
        for sh in (1, 2, 4):
            x = x + jnp.where(sub >= sh, pltpu.roll(x, sh, axis=0), 0.0)
        if carry is not None:
            x = x + carry
        carry = x[7:8]
        outs.append(x)
    return jnp.concatenate(outs, axis=0)


def _block_rel(b, block, ref_fn):
    pieces = []
    for r0 in range(0, b.shape[0], block):
        pieces.append(b[r0:r0 + block] - ref_fn(r0))
    return jnp.concatenate(pieces, axis=0)


def _hgrn_kernel(layer_idx, x_ref, s0_ref, lbl_ref, g_ref, win_ref, gn_ref,
                 wo_ref, out_ref, sfin_ref, st_ref, o_ref):
    j = pl.program_id(1)
    nj = pl.num_programs(1)
    nseq, tl, D = x_ref.shape
    nh = D // HEAD_DIM

    @pl.when(j == 0)
    def _():
        for s in range(nseq):
            for h in range(nh):
                st_ref[s, h] = s0_ref[s, h].T

    logits = lbl_ref[...]
    e = jnp.exp(logits - jnp.max(logits, axis=0, keepdims=True))
    lb = jnp.sum(e[1:layer_idx + 1], axis=0, keepdims=True) / jnp.sum(e, axis=0, keepdims=True)

    x = _rows(x_ref)
    part_rows = min(x.shape[0], HGRN_PART_ROWS)
    nparts = x.shape[0] // part_rows

    npairs = D // MXU_WIDTH
    heads_per_pair = MXU_WIDTH // HEAD_DIM

    def proj(hn, which, c):
        c0 = which * D + c * MXU_WIDTH
        return jnp.dot(hn, win_ref[:, c0:c0 + MXU_WIDTH], preferred_element_type=F32)

    def new_part(p):
        xp = x[p * part_rows:(p + 1) * part_rows]
        return dict(x=xp, hn=_rmsnorm(xp, g_ref[...]).astype(BF16), pairs={})

    def project_steps(part):
        def q_step(c):
            part["pairs"][c] = dict(q=_silu(proj(part["hn"], 0, c)))

        def f_step(c):
            one_minus_lb = 1.0 - lb[:, c * MXU_WIDTH:(c + 1) * MXU_WIDTH]
            kk = one_minus_lb * _sigmoid(-proj(part["hn"], 1, c))
            forget = jnp.maximum(1.0 - kk, FORGET_FLOOR)
            part["pairs"][c].update(kk=kk, glog=jnp.log2(forget))

        def v_step(c):
            part["pairs"][c]["v"] = proj(part["hn"], 2, c).astype(BF16)

        return [functools.partial(step, c) for c in range(npairs)
                for step in (q_step, f_step, v_step)]

    row = lax.broadcasted_iota(jnp.int32, (CHUNK, CHUNK), 0)
    col = lax.broadcasted_iota(jnp.int32, (CHUNK, CHUNK), 1)
    half, blk = CHUNK // 2, CHUNK // 4
    row_blk = jnp.right_shift(row, blk.bit_length() - 1)
    col_blk = jnp.right_shift(col, blk.bit_length() - 1)
    mask2 = ((row >= half) == (col >= half)) & (row_blk > col_blk)
    mask3 = (row_blk == col_blk) & (row >= col)
    nt_dims = (((1,), (1,)), ((), ()))
    tn_dims = (((0,), (0,)), ((), ()))
    zeros_half = jnp.zeros((half, HEAD_DIM), BF16)
    zeros_blk = jnp.zeros((blk, HEAD_DIM), BF16)

    def scores(part, r0, h):
        rs = slice(r0 % part_rows, r0 % part_rows + CHUNK)
        pair = part["pairs"][h // heads_per_pair]
        hs = slice(h % heads_per_pair * HEAD_DIM, (h % heads_per_pair + 1) * HEAD_DIM)
        b = _cumsum_rows(pair["glog"][rs, hs])
        b_end = b[CHUNK - 1:CHUNK]
        qc = pair["q"][rs, hs]
        kc = pair["kk"][rs, hs]
        vc = pair["v"][rs, hs]
        q1 = (qc[half:] * jnp.exp2(b[half:] - b[half:half + 1])).astype(BF16)
        k1 = (kc[:half] * jnp.exp2(b[half:half + 1] - b[:half])).astype(BF16)
        m1 = lax.dot_general(q1, jnp.concatenate([k1, zeros_half], axis=0), nt_dims,
                             preferred_element_type=F32)
        q2, k2 = [], []
        for h0 in (0, half):
            ref = b[h0 + blk:h0 + blk + 1]
            lo, hi = slice(h0, h0 + blk), slice(h0 + blk, h0 + half)
            q2 += [zeros_blk, (qc[hi] * jnp.exp2(b[hi] - ref)).astype(BF16)]
            k2 += [(kc[lo] * jnp.exp2(ref - b[lo])).astype(BF16), zeros_blk]
        m2 = lax.dot_general(jnp.concatenate(q2, axis=0), jnp.concatenate(k2, axis=0),
                             nt_dims, preferred_element_type=F32)
        e3 = _block_rel(b, blk, lambda i: 0.5 * (b[i:i + 1] + b[i + blk - 1:i + blk]))
        q3 = (qc * jnp.exp2(e3)).astype(BF16)
        k3 = (kc * jnp.exp2(-e3)).astype(BF16)
        m3 = lax.dot_general(q3, k3, nt_dims, preferred_element_type=F32)
        a = jnp.where(mask2, m2, jnp.where(mask3, m3, 0.0))
        a = jnp.concatenate([a[:half], a[half:] + m1], axis=0)
        q_in = (qc * jnp.exp2(b)).astype(BF16)
        k_up = (kc * jnp.exp2(b_end - b)).astype(BF16)
        upd = lax.dot_general(vc, k_up, tn_dims, preferred_element_type=F32)
        return a.astype(BF16), q_in, vc, upd, jnp.exp2(b_end)

    def outputs(s, r0, h, a, q_in, vc, upd, dec):
        hs = slice(h * HEAD_DIM, (h + 1) * HEAD_DIM)
        st = st_ref[s, h]
        o_in = lax.dot_general(q_in, st.astype(BF16), nt_dims, preferred_element_type=F32)
        o_ref[r0:r0 + CHUNK, hs] = o_in + jnp.dot(a, vc, preferred_element_type=F32)
        st_ref[s, h] = st * dec + upd

    def gate_pair(p, part, c):
        rows = slice(p * part_rows, (p + 1) * part_rows)
        gate = _silu(proj(part["hn"], 3, c))
        for hp in range(heads_per_pair):
            hs = slice(c * MXU_WIDTH + hp * HEAD_DIM, c * MXU_WIDTH + (hp + 1) * HEAD_DIM)
            oh = o_ref[rows, hs]
            ms = jnp.mean(oh * oh, axis=-1, keepdims=True)
            o_ref[rows, hs] = (oh * lax.rsqrt(ms + EPS) * gn_ref[:, hs]
                               * gate[:, hp * HEAD_DIM:(hp + 1) * HEAD_DIM])

    def out_cols(p, part, c):
        rows = slice(p * part_rows, (p + 1) * part_rows)
        cs = slice(c * MXU_WIDTH, (c + 1) * MXU_WIDTH)
        y = part["x"][:, cs] + jnp.dot(o_ref[rows, :].astype(BF16), wo_ref[:, cs],
                                       preferred_element_type=F32)
        if nseq == 1:
            out_ref[0, rows, cs] = y
        else:
            seqs = slice(p * part_rows // tl, (p + 1) * part_rows // tl)
            out_ref[seqs, :, cs] = y.reshape(part_rows // tl, tl, MXU_WIDTH)

    def finish_steps(p, part):
        steps = [functools.partial(gate_pair, p, part, c) for c in range(npairs)]
        return steps + [functools.partial(out_cols, p, part, c) for c in range(npairs)]

    chunks = [(s, s * tl + c * CHUNK) for s in range(nseq) for c in range(tl // CHUNK)]
    parts = {0: new_part(0)}
    for step in project_steps(parts[0]):
        step()
    pending = [scores(parts[0], chunks[0][1], h) for h in range(nh)]
    for i, (s, r0) in enumerate(chunks):
        p = r0 // part_rows
        if r0 % part_rows == 0 and p + 1 < nparts:
            parts[p + 1] = new_part(p + 1)
            for step in project_steps(parts[p + 1]):
                step()
        ready = pending
        pending = []
        for h in range(nh):
            if i + 1 < len(chunks):
                nxt = chunks[i + 1][1]
                pending.append(scores(parts[nxt // part_rows], nxt, h))
            outputs(s, r0, h, *ready[h])
        if (r0 + CHUNK) % part_rows == 0:
            for step in finish_steps(p, parts.pop(p)):
                step()

    @pl.when(j == nj - 1)
    def _():
        for s in range(nseq):
            for h in range(nh):
                sfin_ref[s, h] = st_ref[s, h].T


def _hgrn_layer(x, state, state_layer, w, layer, jm):
    B, L, D = x.shape
    nh = D // HEAD_DIM
    nseq, tl = _tile_plan(B, L, HGRN_TILE_ROWS)
    out_state_spec = pl.BlockSpec((nseq, nh, HEAD_DIM, HEAD_DIM), lambda b, j: (b, 0, 0, 0))
    return pl.pallas_call(
        functools.partial(_hgrn_kernel, layer),
        grid=(B // nseq, L // tl),
        in_specs=[
            _rows_spec(nseq, tl, D),
            _state_spec(state, state_layer, nseq),
            pl.BlockSpec(w["hgrn_lb_logits"].shape, lambda b, j: (0, 0),
                         pipeline_mode=pl.Buffered(1)),
            _layer_spec(w["norm_mix_g"], layer),
            _layer_spec(w["hgrn_w_in"], jm),
            _layer_spec(w["hgrn_norm_g"], jm),
            _layer_spec(w["hgrn_w_out"], jm),
        ],
        out_specs=[_rows_spec(nseq, tl, D), out_state_spec],
        out_shape=[jax.ShapeDtypeStruct((B, L, D), F32),
                   jax.ShapeDtypeStruct((B, nh, HEAD_DIM, HEAD_DIM), F32)],
        scratch_shapes=[pltpu.VMEM((nseq, nh, HEAD_DIM, HEAD_DIM), F32),
                        pltpu.VMEM((nseq * tl, D), F32)],
        compiler_params=_params(),
        name="hgrn_layer",
    )(x, state, w["hgrn_lb_logits"], w["norm_mix_g"], w["hgrn_w_in"], w["hgrn_norm_g"],
      w["hgrn_w_out"])


def _trunk(x, pool_hist, hgrn_state, conv_hist, fresh, w):
    depth = w["norm_mix_g"].shape[0]
    new_pool, new_hgrn, new_conv = [], [], []
    for i in range(depth):
        jm = i // 2
        final = i == depth - 1
        conv_layer = 0 if fresh else i
        if i % 2 == 0:
            x, p_last, c_last = _pool_ffn_layer(
                x, pool_hist, 0 if fresh else jm, conv_hist, conv_layer, w, i, jm,
                ramp=fresh, final_norm=final)
            new_pool.append(p_last[:, 1:, :])
        else:
            x, s_fin = _hgrn_layer(x, hgrn_state, 0 if fresh else jm, w, i, jm)
            new_hgrn.append(s_fin)
            x, c_last = _ffn_layer(x, conv_hist, conv_layer, w, i, final)
        new_conv.append(c_last[:, CONV_HIST - (CONV_W - 1):, :])
    return x, jnp.stack(new_pool), jnp.stack(new_hgrn), jnp.stack(new_conv)


def kernel(x_prompt, x_sample, state_pool, state_hgrn, state_ffn_conv, norm_mix_g, pool_w, pool_scale, hgrn_w_in, hgrn_lb_logits, hgrn_norm_g, hgrn_w_out, norm_ffn_g, ffn_w_up, ffn_conv_w, ffn_conv_b, ffn_w_down, norm_out_g):
    depth, D = norm_mix_g.shape
    ff = ffn_conv_b.shape[-1]
    nh = D // HEAD_DIM
    Bp = x_prompt.shape[0]
    gw = D // len(POOL_WINDOWS)

    lane_win = jnp.repeat(jnp.asarray(POOL_WINDOWS, F32), gw)[None, :]
    pos = jnp.arange(POOL_HIST, dtype=F32)[:, None] + 1.0
    w = dict(
        norm_mix_g=norm_mix_g[:, None, :], norm_ffn_g=norm_ffn_g[:, None, :],
        norm_out_g=norm_out_g[None, None, :],
        pool_w=pool_w.astype(BF16), pool_scale=pool_scale[:, None, :],
        pool_invw=(1.0 / lane_win)[None], pool_inv16=(1.0 / jnp.minimum(lane_win, pos))[None],
        hgrn_w_in=hgrn_w_in.astype(BF16), hgrn_lb_logits=hgrn_lb_logits,
        hgrn_norm_g=hgrn_norm_g[:, None, :], hgrn_w_out=hgrn_w_out.astype(BF16),
        ffn_w_up=ffn_w_up.astype(BF16), ffn_conv_w=ffn_conv_w,
        ffn_conv_b=ffn_conv_b[:, None, :], ffn_w_down=ffn_w_down.astype(BF16))

    y_p, pool_p, hgrn_p, conv_p = _trunk(
        x_prompt,
        jnp.zeros((1, Bp, POOL_HIST, D), F32),
        jnp.zeros((1, Bp, nh, HEAD_DIM, HEAD_DIM), F32),
        jnp.zeros((1, Bp, CONV_HIST, ff), F32),
        True, w)

    pool_pad = POOL_HIST - state_pool.shape[2]
    conv_pad = CONV_HIST - state_ffn_conv.shape[2]
    y_s, pool_s, hgrn_s, conv_s = _trunk(
        x_sample,
        jnp.pad(state_pool, ((0, 0), (0, 0), (pool_pad, 0), (0, 0))),
        state_hgrn,
        jnp.pad(state_ffn_conv, ((0, 0), (0, 0), (conv_pad, 0), (0, 0))),
        False, w)

    return (y_p, y_s, pool_p, pool_s, hgrn_p, hgrn_s, conv_p, conv_s)
```

```python
import functools
import math

import jax
import jax.numpy as jnp
from jax import lax
from jax.experimental import pallas as pl
from jax.experimental.pallas import tpu as pltpu

F32 = jnp.float32
BF16 = jnp.bfloat16

EPS = 1e-6
POOL_WINDOWS = (2, 4, 8, 16)
POOL_HIST = 16
CONV_HIST = 8
CONV_W = 3
HEAD_DIM = 128
CHUNK = 64
FORGET_FLOOR = 2.0 ** -15
assert -math.log2(FORGET_FLOOR) * (CHUNK // 4) / 2 <= 126
MXU_WIDTH = 256
FFN_CHUNK = MXU_WIDTH
FFN_TILE_ROWS = 512
POOL_BLOCK_ROWS = 512
HGRN_TILE_ROWS = 1024
HGRN_PART_ROWS = 256
VMEM_LIMIT_BYTES = 56 * 1024 * 1024


def _rmsnorm(x, g):
    ms = jnp.mean(x * x, axis=-1, keepdims=True)
    return x * lax.rsqrt(ms + EPS) * g


def _sigmoid(x):
    return 0.5 + 0.5 * jnp.tanh(0.5 * x)


def _silu(x):
    h = 0.5 * x
    return h + h * jnp.tanh(h)


def _tile_plan(batch, seq_len, rows):
    tl = min(seq_len, rows)
    assert seq_len % tl == 0 and tl % CHUNK == 0
    nseq = max(1, min(batch, rows // tl))
    while batch % nseq:
        nseq -= 1
    return nseq, tl


def _layer_spec(arr, layer, *tail_block_idx):
    tail = arr.shape[1:]
    idx = tuple(tail_block_idx) + (0,) * (len(tail) - len(tail_block_idx))
    return pl.BlockSpec((None,) + tuple(tail), lambda b, j: (layer,) + idx,
                        pipeline_mode=pl.Buffered(1))


def _state_spec(arr, layer, nseq):
    tail = arr.shape[2:]
    return pl.BlockSpec((None, nseq) + tuple(tail),
                        lambda b, j: (layer, b) + (0,) * len(tail))


def _rows_spec(nseq, tl, d):
    return pl.BlockSpec((nseq, tl, d), lambda b, j: (b, j, 0))


def _params():
    return pltpu.CompilerParams(
        dimension_semantics=("arbitrary", "arbitrary"),
        vmem_limit_bytes=VMEM_LIMIT_BYTES)


def _pool_steps(x_ref, dst_ref, g_ref, w_ref, scale_ref, invw_ref, inv16_ref, last_ref,
                carry_ref, hn_ref, *, ramp, commit):
    nseq, tl, d_model = x_ref.shape
    gw = w_ref.shape[-1]
    assert all(b == 2 * a for a, b in zip((1,) + POOL_WINDOWS, POOL_WINDOWS))
    assert POOL_WINDOWS[-1] <= POOL_HIST

    seg = min(tl, POOL_BLOCK_ROWS)
    segments = [(s, r0, seg) for s in range(nseq) for r0 in range(0, tl, seg)]
    per_block = max(1, POOL_BLOCK_ROWS // seg)
    blocks = [segments[i:i + per_block] for i in range(0, len(segments), per_block)]

    def norm_step(block):
        for s, r0, n in block:
            rows = slice(s * tl + r0, s * tl + r0 + n)
            hn_ref[rows, :] = _rmsnorm(x_ref[s, r0:r0 + n, :], g_ref[...])

    def group_step(gi, block):
        sl = slice(gi * gw, (gi + 1) * gw)
        for s, r0, n in block:
            first = s * tl + r0
            rows = slice(first, first + n)
            hn = hn_ref[rows, sl]
            prev = carry_ref[s, :, sl] if r0 == 0 else hn_ref[first - POOL_HIST:first, sl]
            level = jnp.concatenate([prev, hn], axis=0)
            for k in range(gi + 1):
                level = level + pltpu.roll(level, 2 ** k, axis=0)
            ws = level[POOL_HIST:]
            d = ws * invw_ref[:, sl] - hn
            if ramp and r0 == 0:
                head = ws[0:POOL_HIST] * inv16_ref[:, sl] - hn[0:POOL_HIST]
                d = jnp.concatenate([head, d[POOL_HIST:]], axis=0)
            y = jnp.dot(d.astype(BF16), w_ref[gi], preferred_element_type=F32)
            dst_ref[rows, sl] = x_ref[s, r0:r0 + n, sl] + y * scale_ref[:, sl]

    def advance():
        for s in range(nseq):
            tail = hn_ref[(s + 1) * tl - POOL_HIST:(s + 1) * tl, :]
            last_ref[s] = tail
            carry_ref[s] = tail

    def commit_step():
        if commit is None:
            advance()
        else:
            pl.when(commit)(advance)

    return ([functools.partial(norm_step, blk) for blk in blocks]
            + [functools.partial(group_step, gi, blk) for blk in blocks
               for gi in range(len(POOL_WINDOWS))]
            + [commit_step])


def _conv_ffn(j, x, hist_ref, g_ref, wg_ref, wv_ref, cw_ref, cb_ref, wd_ref,
              last_ref, gate_ref, hid_ref, side_steps=()):
    side_steps = list(side_steps)
    nseq, tl = gate_ref.shape[0], gate_ref.shape[1] - CONV_HIST
    m, d_model = x.shape
    ff = wg_ref.shape[-1]

    @pl.when(j == 0)
    def _():
        gate_ref[:, 0:CONV_HIST, :] = hist_ref[...]

    hn = _rmsnorm(x, g_ref[...]).astype(BF16)

    for c0 in range(0, ff, FFN_CHUNK):
        sl = slice(c0, c0 + FFN_CHUNK)
        gate = jnp.dot(hn, wg_ref[:, sl], preferred_element_type=F32)
        val = jnp.dot(hn, wv_ref[:, sl], preferred_element_type=F32)
        gate_ref[:, CONV_HIST:CONV_HIST + tl, sl] = gate.reshape(nseq, tl, FFN_CHUNK)
        conv = cb_ref[:, sl] + cw_ref[CONV_W - 1:CONV_W, sl] * gate
        for k in range(1, CONV_W):
            prev = gate_ref[:, CONV_HIST - k:CONV_HIST - k + tl, sl].reshape(m, FFN_CHUNK)
            conv = conv + cw_ref[CONV_W - 1 - k:CONV_W - k, sl] * prev
        hid_ref[:, sl] = (_silu(conv) * val).astype(BF16)

    tail = gate_ref[:, tl:tl + CONV_HIST, :]
    last_ref[...] = tail
    gate_ref[:, 0:CONV_HIST, :] = tail
    hid = hid_ref[...]
    n_blocks = d_model // MXU_WIDTH
    cols = []
    for c in range(n_blocks):
        cols.append(jnp.dot(hid, wd_ref[:, c * MXU_WIDTH:(c + 1) * MXU_WIDTH],
                            preferred_element_type=F32))
        for _ in range(-(-len(side_steps) // (n_blocks - c))):
            side_steps.pop(0)()
    return x + jnp.concatenate(cols, axis=1)


def _rows(x_ref):
    x = x_ref[...]
    return x.reshape(-1, x.shape[-1])


def _store_rows(out_ref, y, final_norm, gout_ref):
    if final_norm:
        y = _rmsnorm(y, gout_ref[...])
    out_ref[...] = y.reshape(out_ref.shape)


def _ffn_kernel(final_norm, x_ref, chist_ref, gffn_ref, wg_ref, wv_ref, cw_ref,
                cb_ref, wd_ref, gout_ref, out_ref, clast_ref, gate_ref, hid_ref):
    j = pl.program_id(1)
    y = _conv_ffn(j, _rows(x_ref), chist_ref, gffn_ref, wg_ref, wv_ref, cw_ref,
                  cb_ref, wd_ref, clast_ref, gate_ref, hid_ref)
    _store_rows(out_ref, y, final_norm, gout_ref)


def _pool_ffn_kernel(ramp, final_norm, lookahead, xnext_ref, xfirst_ref, phist_ref, gmix_ref,
                     pw_ref, pscale_ref, invw_ref, inv16_ref, chist_ref, gffn_ref,
                     wg_ref, wv_ref, cw_ref, cb_ref, wd_ref, gout_ref, out_ref,
                     plast_ref, clast_ref, carry_ref, hn_ref, mixed_ref, gate_ref, hid_ref):
    j = pl.program_id(1)
    nj = pl.num_programs(1)
    pool = functools.partial(_pool_steps, g_ref=gmix_ref, w_ref=pw_ref, scale_ref=pscale_ref,
                             invw_ref=invw_ref, inv16_ref=inv16_ref, last_ref=plast_ref,
                             carry_ref=carry_ref, hn_ref=hn_ref)

    @pl.when(j == 0)
    def _():
        carry_ref[...] = phist_ref[...]
        for step in pool(xfirst_ref, mixed_ref.at[0], ramp=ramp, commit=None):
            step()

    slot = lax.rem(j, 2)
    pool_next = (pool(xnext_ref, mixed_ref.at[1 - slot], ramp=False, commit=j + 1 < nj)
                 if lookahead else ())
    y = _conv_ffn(j, mixed_ref[slot], chist_ref, gffn_ref, wg_ref, wv_ref, cw_ref, cb_ref,
                  wd_ref, clast_ref, gate_ref, hid_ref, side_steps=pool_next)
    _store_rows(out_ref, y, final_norm, gout_ref)


def _ffn_operands(w, layer, conv_hist, conv_layer, nseq):
    ff = w["ffn_conv_b"].shape[-1]
    args = [conv_hist, w["norm_ffn_g"], w["ffn_w_up"], w["ffn_w_up"], w["ffn_conv_w"],
            w["ffn_conv_b"], w["ffn_w_down"], w["norm_out_g"]]
    d_model = w["norm_out_g"].shape[-1]
    specs = [
        _state_spec(conv_hist, conv_layer, nseq),
        _layer_spec(w["norm_ffn_g"], layer),
        pl.BlockSpec((None, d_model, ff), lambda b, j: (layer, 0, 0),
                     pipeline_mode=pl.Buffered(1)),
        pl.BlockSpec((None, d_model, ff), lambda b, j: (layer, 0, 1),
                     pipeline_mode=pl.Buffered(1)),
        _layer_spec(w["ffn_conv_w"], layer),
        _layer_spec(w["ffn_conv_b"], layer),
        _layer_spec(w["ffn_w_down"], layer),
        _layer_spec(w["norm_out_g"], 0),
    ]
    return args, specs


def _ffn_scratch(nseq, tl, ff):
    return [pltpu.VMEM((nseq, tl + CONV_HIST, ff), F32),
            pltpu.VMEM((nseq * tl, ff), BF16)]


def _ffn_layer(x, conv_hist, conv_layer, w, layer, final_norm):
    B, L, D = x.shape
    ff = w["ffn_conv_b"].shape[-1]
    nseq, tl = _tile_plan(B, L, FFN_TILE_ROWS)
    args, specs = _ffn_operands(w, layer, conv_hist, conv_layer, nseq)
    return pl.pallas_call(
        functools.partial(_ffn_kernel, final_norm),
        grid=(B // nseq, L // tl),
        in_specs=[_rows_spec(nseq, tl, D)] + specs,
        out_specs=[_rows_spec(nseq, tl, D),
                   pl.BlockSpec((nseq, CONV_HIST, ff), lambda b, j: (b, 0, 0))],
        out_shape=[jax.ShapeDtypeStruct((B, L, D), F32),
                   jax.ShapeDtypeStruct((B, CONV_HIST, ff), F32)],
        scratch_shapes=_ffn_scratch(nseq, tl, ff),
        compiler_params=_params(),
        name="ffn_layer",
    )(x, *args)


def _pool_ffn_layer(x, pool_hist, pool_layer, conv_hist, conv_layer, w, layer, jm,
                    ramp, final_norm):
    B, L, D = x.shape
    ff = w["ffn_conv_b"].shape[-1]
    nseq, tl = _tile_plan(B, L, FFN_TILE_ROWS)
    ffn_args, ffn_specs = _ffn_operands(w, layer, conv_hist, conv_layer, nseq)
    args = [pool_hist, w["norm_mix_g"], w["pool_w"], w["pool_scale"], w["pool_invw"],
            w["pool_inv16"]] + ffn_args
    specs = [
        _state_spec(pool_hist, pool_layer, nseq),
        _layer_spec(w["norm_mix_g"], layer),
        _layer_spec(w["pool_w"], jm),
        _layer_spec(w["pool_scale"], jm),
        _layer_spec(w["pool_invw"], 0),
        _layer_spec(w["pool_inv16"], 0),
    ] + ffn_specs
    nt = L // tl
    return pl.pallas_call(
        functools.partial(_pool_ffn_kernel, ramp, final_norm, nt > 1),
        grid=(B // nseq, nt),
        in_specs=[
            pl.BlockSpec((nseq, tl, D), lambda b, j: (b, jnp.minimum(j + 1, nt - 1), 0)),
            pl.BlockSpec((nseq, tl, D), lambda b, j: (b, 0, 0)),
        ] + specs,
        out_specs=[_rows_spec(nseq, tl, D),
                   pl.BlockSpec((nseq, POOL_HIST, D), lambda b, j: (b, 0, 0)),
                   pl.BlockSpec((nseq, CONV_HIST, ff), lambda b, j: (b, 0, 0))],
        out_shape=[jax.ShapeDtypeStruct((B, L, D), F32),
                   jax.ShapeDtypeStruct((B, POOL_HIST, D), F32),
                   jax.ShapeDtypeStruct((B, CONV_HIST, ff), F32)],
        scratch_shapes=[pltpu.VMEM((nseq, POOL_HIST, D), F32),
                        pltpu.VMEM((nseq * tl, D), F32),
                        pltpu.VMEM((2, nseq * tl, D), F32),
                        ] + _ffn_scratch(nseq, tl, ff),
        compiler_params=_params(),
        name="pool_ffn_layer",
    )(x, x, *args)


def _cumsum_rows(g):
    rows, width = g.shape
    sub = lax.broadcasted_iota(jnp.int32, (8, width), 0)
    outs = []
    carry = None
    for r0 in range(0, rows, 8):
        x = g[r0:r0 + 8]
        for sh in (1, 2, 4):
            x = x + jnp.where(sub >= sh, pltpu.roll(x, sh, axis=0), 0.0)
        if carry is not None:
            x = x + carry
        carry = x[7:8]
        outs.append(x)
    return jnp.concatenate(outs, axis=0)


def _block_rel(b, block, ref_fn):
    pieces = []
    for r0 in range(0, b.shape[0], block):
        pieces.append(b[r0:r0 + block] - ref_fn(r0))
    return jnp.concatenate(pieces, axis=0)


def _hgrn_kernel(layer_idx, x_ref, s0_ref, lbl_ref, g_ref, win_ref, gn_ref,
                 wo_ref, out_ref, sfin_ref, st_ref, o_ref):
    j = pl.program_id(1)
    nj = pl.num_programs(1)
    nseq, tl, D = x_ref.shape
    nh = D // HEAD_DIM

    @pl.when(j == 0)
    def _():
        for s in range(nseq):
            for h in range(nh):
                st_ref[s, h] = s0_ref[s, h].T

    logits = lbl_ref[...]
    e = jnp.exp(logits - jnp.max(logits, axis=0, keepdims=True))
    lb = jnp.sum(e[1:layer_idx + 1], axis=0, keepdims=True) / jnp.sum(e, axis=0, keepdims=True)

    x = _rows(x_ref)
    part_rows = min(x.shape[0], HGRN_PART_ROWS)
    nparts = x.shape[0] // part_rows

    npairs = D // MXU_WIDTH
    heads_per_pair = MXU_WIDTH // HEAD_DIM

    def proj(hn, which, c):
        c0 = which * D + c * MXU_WIDTH
        return jnp.dot(hn, win_ref[:, c0:c0 + MXU_WIDTH], preferred_element_type=F32)

    def new_part(p):
        xp = x[p * part_rows:(p + 1) * part_rows]
        return dict(x=xp, hn=_rmsnorm(xp, g_ref[...]).astype(BF16), pairs={})

    def project_steps(part):
        def q_step(c):
            part["pairs"][c] = dict(q=_silu(proj(part["hn"], 0, c)))

        def f_step(c):
            one_minus_lb = 1.0 - lb[:, c * MXU_WIDTH:(c + 1) * MXU_WIDTH]
            kk = one_minus_lb * _sigmoid(-proj(part["hn"], 1, c))
            forget = jnp.maximum(1.0 - kk, FORGET_FLOOR)
            part["pairs"][c].update(kk=kk, glog=jnp.log2(forget))

        def v_step(c):
            part["pairs"][c]["v"] = proj(part["hn"], 2, c).astype(BF16)

        return [functools.partial(step, c) for c in range(npairs)
                for step in (q_step, f_step, v_step)]

    row = lax.broadcasted_iota(jnp.int32, (CHUNK, CHUNK), 0)
    col = lax.broadcasted_iota(jnp.int32, (CHUNK, CHUNK), 1)
    half, blk = CHUNK // 2, CHUNK // 4
    row_blk = jnp.right_shift(row, blk.bit_length() - 1)
    col_blk = jnp.right_shift(col, blk.bit_length() - 1)
    mask2 = ((row >= half) == (col >= half)) & (row_blk > col_blk)
    mask3 = (row_blk == col_blk) & (row >= col)
    nt_dims = (((1,), (1,)), ((), ()))
    tn_dims = (((0,), (0,)), ((), ()))
    zeros_half = jnp.zeros((half, HEAD_DIM), BF16)
    zeros_blk = jnp.zeros((blk, HEAD_DIM), BF16)

    def scores(part, r0, h):
        rs = slice(r0 % part_rows, r0 % part_rows + CHUNK)
        pair = part["pairs"][h // heads_per_pair]
        hs = slice(h % heads_per_pair * HEAD_DIM, (h % heads_per_pair + 1) * HEAD_DIM)
        b = _cumsum_rows(pair["glog"][rs, hs])
        b_end = b[CHUNK - 1:CHUNK]
        qc = pair["q"][rs, hs]
        kc = pair["kk"][rs, hs]
        vc = pair["v"][rs, hs]
        q1 = (qc[half:] * jnp.exp2(b[half:] - b[half:half + 1])).astype(BF16)
        k1 = (kc[:half] * jnp.exp2(b[half:half + 1] - b[:half])).astype(BF16)
        m1 = lax.dot_general(q1, jnp.concatenate([k1, zeros_half], axis=0), nt_dims,
                             preferred_element_type=F32)
        q2, k2 = [], []
        for h0 in (0, half):
            ref = b[h0 + blk:h0 + blk + 1]
            lo, hi = slice(h0, h0 + blk), slice(h0 + blk, h0 + half)
            q2 += [zeros_blk, (qc[hi] * jnp.exp2(b[hi] - ref)).astype(BF16)]
            k2 += [(kc[lo] * jnp.exp2(ref - b[lo])).astype(BF16), zeros_blk]
        m2 = lax.dot_general(jnp.concatenate(q2, axis=0), jnp.concatenate(k2, axis=0),
                             nt_dims, preferred_element_type=F32)
        e3 = _block_rel(b, blk, lambda i: 0.5 * (b[i:i + 1] + b[i + blk - 1:i + blk]))
        q3 = (qc * jnp.exp2(e3)).astype(BF16)
        k3 = (kc * jnp.exp2(-e3)).astype(BF16)
        m3 = lax.dot_general(q3, k3, nt_dims, preferred_element_type=F32)
        a = jnp.where(mask2, m2, jnp.where(mask3, m3, 0.0))
        a = jnp.concatenate([a[:half], a[half:] + m1], axis=0)
        q_in = (qc * jnp.exp2(b)).astype(BF16)
        k_up = (kc * jnp.exp2(b_end - b)).astype(BF16)
        upd = lax.dot_general(vc, k_up, tn_dims, preferred_element_type=F32)
        return a.astype(BF16), q_in, vc, upd, jnp.exp2(b_end)

    def outputs(s, r0, h, a, q_in, vc, upd, dec):
        hs = slice(h * HEAD_DIM, (h + 1) * HEAD_DIM)
        st = st_ref[s, h]
        o_in = lax.dot_general(q_in, st.astype(BF16), nt_dims, preferred_element_type=F32)
        o_ref[r0:r0 + CHUNK, hs] = o_in + jnp.dot(a, vc, preferred_element_type=F32)
        st_ref[s, h] = st * dec + upd

    def gate_pair(p, part, c):
        rows = slice(p * part_rows, (p + 1) * part_rows)
        gate = _silu(proj(part["hn"], 3, c))
        for hp in range(heads_per_pair):
            hs = slice(c * MXU_WIDTH + hp * HEAD_DIM, c * MXU_WIDTH + (hp + 1) * HEAD_DIM)
            oh = o_ref[rows, hs]
            ms = jnp.mean(oh * oh, axis=-1, keepdims=True)
            o_ref[rows, hs] = (oh * lax.rsqrt(ms + EPS) * gn_ref[:, hs]
                               * gate[:, hp * HEAD_DIM:(hp + 1) * HEAD_DIM])

    def out_cols(p, part, c):
        rows = slice(p * part_rows, (p + 1) * part_rows)
        cs = slice(c * MXU_WIDTH, (c + 1) * MXU_WIDTH)
        y = part["x"][:, cs] + jnp.dot(o_ref[rows, :].astype(BF16), wo_ref[:, cs],
                                       preferred_element_type=F32)
        if nseq == 1:
            out_ref[0, rows, cs] = y
        else:
            seqs = slice(p * part_rows // tl, (p + 1) * part_rows // tl)
            out_ref[seqs, :, cs] = y.reshape(part_rows // tl, tl, MXU_WIDTH)

    def finish_steps(p, part):
        steps = [functools.partial(gate_pair, p, part, c) for c in range(npairs)]
        return steps + [functools.partial(out_cols, p, part, c) for c in range(npairs)]

    chunks = [(s, s * tl + c * CHUNK) for s in range(nseq) for c in range(tl // CHUNK)]
    parts = {0: new_part(0)}
    for step in project_steps(parts[0]):
        step()
    pending = [scores(parts[0], chunks[0][1], h) for h in range(nh)]
    for i, (s, r0) in enumerate(chunks):
        p = r0 // part_rows
        if r0 % part_rows == 0 and p + 1 < nparts:
            parts[p + 1] = new_part(p + 1)
            for step in project_steps(parts[p + 1]):
                step()
        ready = pending
        pending = []
        for h in range(nh):
            if i + 1 < len(chunks):
                nxt = chunks[i + 1][1]
                pending.append(scores(parts[nxt // part_rows], nxt, h))
            outputs(s, r0, h, *ready[h])
        if (r0 + CHUNK) % part_rows == 0:
            for step in finish_steps(p, parts.pop(p)):
                step()

    @pl.when(j == nj - 1)
    def _():
        for s in range(nseq):
            for h in range(nh):
                sfin_ref[s, h] = st_ref[s, h].T


def _hgrn_layer(x, state, state_layer, w, layer, jm):
    B, L, D = x.shape
    nh = D // HEAD_DIM
    nseq, tl = _tile_plan(B, L, HGRN_TILE_ROWS)
    out_state_spec = pl.BlockSpec((nseq, nh, HEAD_DIM, HEAD_DIM), lambda b, j: (b, 0, 0, 0))
    return pl.pallas_call(
        functools.partial(_hgrn_kernel, layer),
        grid=(B // nseq, L // tl),
        in_specs=[
            _rows_spec(nseq, tl, D),
            _state_spec(state, state_layer, nseq),
            pl.BlockSpec(w["hgrn_lb_logits"].shape, lambda b, j: (0, 0),
                         pipeline_mode=pl.Buffered(1)),
            _layer_spec(w["norm_mix_g"], layer),
            _layer_spec(w["hgrn_w_in"], jm),
            _layer_spec(w["hgrn_norm_g"], jm),
            _layer_spec(w["hgrn_w_out"], jm),
        ],
        out_specs=[_rows_spec(nseq, tl, D), out_state_spec],
        out_shape=[jax.ShapeDtypeStruct((B, L, D), F32),
                   jax.ShapeDtypeStruct((B, nh, HEAD_DIM, HEAD_DIM), F32)],
        scratch_shapes=[pltpu.VMEM((nseq, nh, HEAD_DIM, HEAD_DIM), F32),
                        pltpu.VMEM((nseq * tl, D), F32)],
        compiler_params=_params(),
        name="hgrn_layer",
    )(x, state, w["hgrn_lb_logits"], w["norm_mix_g"], w["hgrn_w_in"], w["hgrn_norm_g"],
      w["hgrn_w_out"])


def _trunk(x, pool_hist, hgrn_state, conv_hist, fresh, w):
    depth = w["norm_mix_g"].shape[0]
    new_pool, new_hgrn, new_conv = [], [], []
    for i in range(depth):
        jm = i // 2
        final = i == depth - 1
        conv_layer = 0 if fresh else i
        if i % 2 == 0:
            x, p_last, c_last = _pool_ffn_layer(
                x, pool_hist, 0 if fresh else jm, conv_hist, conv_layer, w, i, jm,
                ramp=fresh, final_norm=final)
            new_pool.append(p_last[:, 1:, :])
        else:
            x, s_fin = _hgrn_layer(x, hgrn_state, 0 if fresh else jm, w, i, jm)
            new_hgrn.append(s_fin)
            x, c_last = _ffn_layer(x, conv_hist, conv_layer, w, i, final)
        new_conv.append(c_last[:, CONV_HIST - (CONV_W - 1):, :])
    return x, jnp.stack(new_pool), jnp.stack(new_hgrn), jnp.stack(new_conv)


def kernel(x_prompt, x_sample, state_pool, state_hgrn, state_ffn_conv, norm_mix_g, pool_w, pool_scale, hgrn_w_in, hgrn_lb_logits, hgrn_norm_g, hgrn_w_out, norm_ffn_g, ffn_w_up, ffn_conv_w, ffn_conv_b, ffn_w_down, norm_out_g):
    depth, D = norm_mix_g.shape
    ff = ffn_conv_b.shape[-1]
    nh = D // HEAD_DIM
    Bp = x_prompt.shape[0]
    gw = D // len(POOL_WINDOWS)

    lane_win = jnp.repeat(jnp.asarray(POOL_WINDOWS, F32), gw)[None, :]
    pos = jnp.arange(POOL_HIST, dtype=F32)[:, None] + 1.0
    w = dict(
        norm_mix_g=norm_mix_g[:, None, :], norm_ffn_g=norm_ffn_g[:, None, :],
        norm_out_g=norm_out_g[None, None, :],
        pool_w=pool_w.astype(BF16), pool_scale=pool_scale[:, None, :],
        pool_invw=(1.0 / lane_win)[None], pool_inv16=(1.0 / jnp.minimum(lane_win, pos))[None],
        hgrn_w_in=hgrn_w_in.astype(BF16), hgrn_lb_logits=hgrn_lb_logits,
        hgrn_norm_g=hgrn_norm_g[:, None, :], hgrn_w_out=hgrn_w_out.astype(BF16),
        ffn_w_up=ffn_w_up.astype(BF16), ffn_conv_w=ffn_conv_w,
        ffn_conv_b=ffn_conv_b[:, None, :], ffn_w_down=ffn_w_down.astype(BF16))

    y_p, pool_p, hgrn_p, conv_p = _trunk(
        x_prompt,
        jnp.zeros((1, Bp, POOL_HIST, D), F32),
        jnp.zeros((1, Bp, nh, HEAD_DIM, HEAD_DIM), F32),
        jnp.zeros((1, Bp, CONV_HIST, ff), F32),
        True, w)

    pool_pad = POOL_HIST - state_pool.shape[2]
    conv_pad = CONV_HIST - state_ffn_conv.shape[2]
    y_s, pool_s, hgrn_s, conv_s = _trunk(
        x_sample,
        jnp.pad(state_pool, ((0, 0), (0, 0), (pool_pad, 0), (0, 0))),
        state_hgrn,
        jnp.pad(state_ffn_conv, ((0, 0), (0, 0), (conv_pad, 0), (0, 0))),
        False, w)

    return (y_p, y_s, pool_p, pool_s, hgrn_p, hgrn_s, conv_p, conv_s)
```
